```python
import math
import jax, jax.numpy as jnp
from jax import lax
import numpy as np

D_MODEL = 1024
BATCH = 8
SEQ = 2048
DEPTH = 1
DEC_BATCH = 8
DEC_SEQ = 64
PAST_LEN = 1024

CHUNK = 64
EPS = 1e-6
DA_HEADS = 8
DA_HEAD_DIM = 64
DA_V_DIM = 2 * DA_HEAD_DIM
ROPE_THETA = 10000.0
Q_BLOCK = 128
GLA_HEADS = 4
GLA_DK = D_MODEL // 2 // GLA_HEADS
GLA_DV = D_MODEL // GLA_HEADS
GLA_GATE_RANK = 16
GLA_GATE_TAU = 16.0
PEER_HEADS = 8
PEER_N_KEYS = 128
PEER_N_EXPERTS = PEER_N_KEYS * PEER_N_KEYS
PEER_QDIM = 256
PEER_TOPK = 16
PEER_TOKEN_BLOCK = 128

DA_QK_W = DA_HEADS * 2 * DA_HEAD_DIM
DA_V_W = DA_HEADS * DA_V_DIM
GLA_QK_W = GLA_HEADS * GLA_DK
GLA_V_W = GLA_HEADS * GLA_DV
IN_SPLITS = (DA_QK_W, DA_QK_W, DA_V_W, GLA_QK_W, GLA_QK_W, GLA_V_W, GLA_GATE_RANK, GLA_V_W, D_MODEL, D_MODEL)
IN_WIDTH = 2 * DA_QK_W + DA_V_W + 2 * GLA_QK_W + 2 * GLA_V_W + GLA_GATE_RANK + 2 * D_MODEL

kernel_name = "diffattn_gla_peer_streaming_step"


def _rms(x, gain):
    xf = x.astype(jnp.float32)
    y = xf * lax.rsqrt(jnp.mean(xf * xf, axis=-1, keepdims=True) + EPS)
    return (y * gain.astype(jnp.float32)).astype(x.dtype)


def _split_cols(t):
    outs, start = [], 0
    for w in IN_SPLITS:
        outs.append(t[..., start:start + w])
        start += w
    return outs


def _rope(x, pos):
    half = DA_HEAD_DIM // 2
    inv_freq = ROPE_THETA ** (-jnp.arange(half, dtype=jnp.float32) / half)
    ang = pos.astype(jnp.float32)[:, None] * inv_freq[None, :]
    cos = jnp.cos(ang)[:, None, None, :]
    sin = jnp.sin(ang)[:, None, None, :]
    xf = x.astype(jnp.float32)
    x1, x2 = xf[..., :half], xf[..., half:]
    return jnp.concatenate([x1 * cos - x2 * sin, x2 * cos + x1 * sin], axis=-1).astype(x.dtype)


def _diff_attn(q, k, v, q_pos, k_pos, lam):
    s = jnp.einsum('bqhmd,bkhmd->bhmqk', q, k).astype(jnp.float32) * (DA_HEAD_DIM ** -0.5)
    allowed = (k_pos[None, :] // CHUNK) <= (q_pos[:, None] // CHUNK)
    s = jnp.where(allowed, s, -jnp.inf)
    p = jax.nn.softmax(s, axis=-1)
    w = p[:, :, 0] - lam * p[:, :, 1]
    return jnp.einsum('bhqk,bkhd->bqhd', w.astype(v.dtype), v)


def _diff_attn_prompt(q, k, v, lam):
    B, L = q.shape[0], q.shape[1]
    nb = L // Q_BLOCK
    qb = q.reshape(B, nb, Q_BLOCK, DA_HEADS, 2, DA_HEAD_DIM).swapaxes(0, 1)
    pos = jnp.arange(L)
    qpos = pos.reshape(nb, Q_BLOCK)
    out = lax.map(lambda a: _diff_attn(a[0], k, v, a[1], pos, lam), (qb, qpos))
    return out.swapaxes(0, 1).reshape(B, L, DA_HEADS, DA_V_DIM)


def _gla(q, k, v, log_a, s0):
    B, L = q.shape[0], q.shape[1]
    c = CHUNK if L % CHUNK == 0 else L
    n = L // c

    def to_chunks(t):
        t = t.astype(jnp.float32)
        return t.reshape(B, n, c, t.shape[2], t.shape[3]).transpose(1, 0, 3, 2, 4)

    qc = to_chunks(q * (GLA_DK ** -0.5))
    kc, vc, ac = to_chunks(k), to_chunks(v), to_chunks(log_a)
    causal = jnp.tril(jnp.ones((c, c), dtype=bool))[:, :, None]

    def step(S, inp):
        qi, ki, vi, ai = inp
        b = jnp.cumsum(ai, axis=2)
        diff = b[:, :, :, None, :] - b[:, :, None, :, :]
        decay = jnp.exp(jnp.where(causal, diff, -jnp.inf))
        A = jnp.einsum('bhid,bhjd,bhijd->bhij', qi, ki, decay)
        o = jnp.einsum('bhid,bhdv->bhiv', qi * jnp.exp(b), S) + jnp.einsum('bhij,bhjv->bhiv', A, vi)
        b_last = b[:, :, -1:, :]
        S = S * jnp.exp(b_last)[:, :, 0, :, None] + jnp.einsum('bhjd,bhjv->bhdv', ki * jnp.exp(b_last - b), vi)
        return S, o

    S, o = lax.scan(step, s0.astype(jnp.float32), (qc, kc, vc, ac))
    o = o.transpose(1, 0, 3, 2, 4).reshape(B, L, GLA_HEADS, GLA_DV)
    return o.astype(q.dtype), S.astype(q.dtype)


def _peer(x, w_q, sub_keys, expert_u, expert_v):
    B, L, D = x.shape
    T = B * L
    nb = -(-T // PEER_TOKEN_BLOCK)
    xt = jnp.pad(x.reshape(T, D), ((0, nb * PEER_TOKEN_BLOCK - T), (0, 0))).reshape(nb, PEER_TOKEN_BLOCK, D)

    def block(xb):
        q = (xb @ w_q).reshape(PEER_TOKEN_BLOCK, PEER_HEADS, 2, PEER_QDIM // 2)
        s = jnp.einsum('thpd,hpnd->thpn', q, sub_keys).astype(jnp.float32)
        sv, si = lax.top_k(s, PEER_TOPK)
        cand = sv[..., 0, :, None] + sv[..., 1, None, :]
        cidx = si[..., 0, :, None] * PEER_N_KEYS + si[..., 1, None, :]
        top_s, top_pos = lax.top_k(cand.reshape(PEER_TOKEN_BLOCK, PEER_HEADS, PEER_TOPK * PEER_TOPK), PEER_TOPK)
        idx = jnp.take_along_axis(cidx.reshape(PEER_TOKEN_BLOCK, PEER_HEADS, PEER_TOPK * PEER_TOPK), top_pos, axis=-1)
        g = jax.nn.softmax(top_s, axis=-1)
        act = jax.nn.gelu(jnp.einsum('thkd,td->thk', expert_u[idx], xb).astype(jnp.float32), approximate=False)
        return jnp.einsum('thk,thkd->td', (g * act).astype(xb.dtype), expert_v[idx])

    y = lax.map(block, xt).reshape(nb * PEER_TOKEN_BLOCK, D)[:T]
    return y.reshape(B, L, D)


def _layer(x, cache_k, cache_v, s0, p, lam_init):
    B, L, _ = x.shape
    past = 0 if cache_k is None else cache_k.shape[1]
    xn = _rms(x, p['norm1_g'])
    dq, dk, dv, gq, gk, gv, glr, gr, ga, gb = _split_cols(xn @ p['w_in'])
    pos = past + jnp.arange(L)

    lq = p['da_lambda'].astype(jnp.float32)
    lam = jnp.exp(jnp.sum(lq[0] * lq[1])) - jnp.exp(jnp.sum(lq[2] * lq[3])) + lam_init
    q = _rope(_rms(dq.reshape(B, L, DA_HEADS, 2, DA_HEAD_DIM), p['qk_norm_g'][0]), pos)
    k = _rope(_rms(dk.reshape(B, L, DA_HEADS, 2, DA_HEAD_DIM), p['qk_norm_g'][1]), pos)
    v = dv.reshape(B, L, DA_HEADS, DA_V_DIM)
    if cache_k is None:
        o_a = _diff_attn_prompt(q, k, v, lam)
    else:
        kk = jnp.concatenate([cache_k.astype(k.dtype), k], axis=1)
        vv = jnp.concatenate([cache_v.astype(v.dtype), v], axis=1)
        o_a = _diff_attn(q, kk, vv, pos, jnp.arange(past + L), lam)
    o_a = (_rms(o_a, p['da_subln_g']) * (1.0 - lam_init)).reshape(B, L, DA_V_W)

    log_a = jax.nn.log_sigmoid((glr @ p['gla_w_gate2'] + p['gla_b_gate']).astype(jnp.float32)) / GLA_GATE_TAU
    if s0 is None:
        s0 = jnp.zeros((B, GLA_HEADS, GLA_DK, GLA_DV), jnp.float32)
    o_b, s_new = _gla(gq.reshape(B, L, GLA_HEADS, GLA_DK), gk.reshape(B, L, GLA_HEADS, GLA_DK),
                      gv.reshape(B, L, GLA_HEADS, GLA_DV), log_a.reshape(B, L, GLA_HEADS, GLA_DK), s0)
    o_b = _rms(o_b, p['gla_norm_g']).reshape(B, L, GLA_V_W) * jax.nn.silu(gr)

    mix = jax.nn.sigmoid(ga) * (o_a @ p['w_branch_a']) + jax.nn.sigmoid(gb) * (o_b @ p['w_branch_b'])
    h = x + mix @ p['w_out']
    y = h + _peer(_rms(h, p['norm2_g']), p['peer_w_q'], p['peer_sub_keys'], p['peer_u'], p['peer_v'])
    return y, k, v, s_new


def setup_inputs(seed: int = 0) -> dict:
    key = jax.random.key(seed)
    ks = jax.random.split(key, 24)
    f32 = jnp.float32

    def nrm(k, shape, scale):
        return jax.random.normal(k, shape, f32) * scale

    return {
        'x_prompt': nrm(ks[0], (BATCH, SEQ, D_MODEL), 1.0),
        'x_sample': nrm(ks[1], (DEC_BATCH, DEC_SEQ, D_MODEL), 1.0),
        'cache_k': nrm(ks[2], (DEPTH, DEC_BATCH, PAST_LEN, DA_HEADS, 2, DA_HEAD_DIM), 1.0),
        'cache_v': nrm(ks[3], (DEPTH, DEC_BATCH, PAST_LEN, DA_HEADS, DA_V_DIM), 1.0),
        'state_gla': nrm(ks[4], (DEPTH, DEC_BATCH, GLA_HEADS, GLA_DK, GLA_DV), 0.5),
        'norm1_g': 1.0 + nrm(ks[5], (DEPTH, D_MODEL), 0.02),
        'w_in': nrm(ks[6], (DEPTH, D_MODEL, IN_WIDTH), D_MODEL ** -0.5),
        'qk_norm_g': 1.0 + nrm(ks[7], (DEPTH, 2, DA_HEAD_DIM), 0.02),
        'da_lambda': nrm(ks[8], (DEPTH, 4, DA_HEAD_DIM), 0.1),
        'da_subln_g': 1.0 + nrm(ks[9], (DEPTH, DA_V_DIM), 0.02),
        'gla_w_gate2': nrm(ks[10], (DEPTH, GLA_GATE_RANK, GLA_QK_W), GLA_GATE_RANK ** -0.5),
        'gla_b_gate': nrm(ks[11], (DEPTH, GLA_QK_W), 0.1),
        'gla_norm_g': 1.0 + nrm(ks[12], (DEPTH, GLA_DV), 0.02),
        'w_branch_a': nrm(ks[13], (DEPTH, DA_V_W, D_MODEL), DA_V_W ** -0.5),
        'w_branch_b': nrm(ks[14], (DEPTH, GLA_V_W, D_MODEL), GLA_V_W ** -0.5),
        'w_out': nrm(ks[15], (DEPTH, D_MODEL, D_MODEL), D_MODEL ** -0.5),
        'norm2_g': 1.0 + nrm(ks[16], (DEPTH, D_MODEL), 0.02),
        'peer_w_q': nrm(ks[17], (DEPTH, D_MODEL, PEER_HEADS * PEER_QDIM), D_MODEL ** -0.5),
        'peer_sub_keys': nrm(ks[18], (DEPTH, PEER_HEADS, 2, PEER_N_KEYS, PEER_QDIM // 2), (PEER_QDIM // 2) ** -0.5),
        'peer_u': nrm(ks[19], (DEPTH, PEER_N_EXPERTS, D_MODEL), D_MODEL ** -0.5),
        'peer_v': nrm(ks[20], (DEPTH, PEER_N_EXPERTS, D_MODEL), PEER_HEADS ** -0.5),
    }


def reference(x_prompt, x_sample, cache_k, cache_v, state_gla, norm1_g, w_in, qk_norm_g, da_lambda,
              da_subln_g, gla_w_gate2, gla_b_gate, gla_norm_g, w_branch_a, w_branch_b, w_out, norm2_g,
              peer_w_q, peer_sub_keys, peer_u, peer_v):
    yp, ys = x_prompt, x_sample
    kp_l, vp_l, sp_l, ks_l, vs_l, ss_l = [], [], [], [], [], []
    for l in range(DEPTH):
        p = {
            'norm1_g': norm1_g[l], 'w_in': w_in[l], 'qk_norm_g': qk_norm_g[l], 'da_lambda': da_lambda[l],
            'da_subln_g': da_subln_g[l], 'gla_w_gate2': gla_w_gate2[l], 'gla_b_gate': gla_b_gate[l],
            'gla_norm_g': gla_norm_g[l], 'w_branch_a': w_branch_a[l], 'w_branch_b': w_branch_b[l],
            'w_out': w_out[l], 'norm2_g': norm2_g[l], 'peer_w_q': peer_w_q[l],
            'peer_sub_keys': peer_sub_keys[l], 'peer_u': peer_u[l], 'peer_v': peer_v[l],
        }
        lam_init = 0.8 - 0.6 * math.exp(-0.3 * l)
        yp, kp, vp, sp = _layer(yp, None, None, None, p, lam_init)
        ys, kn, vn, sn = _layer(ys, cache_k[l], cache_v[l], state_gla[l], p, lam_init)
        kp_l.append(kp); vp_l.append(vp); sp_l.append(sp)
        ks_l.append(kn); vs_l.append(vn); ss_l.append(sn)
    return (yp, ys, jnp.stack(kp_l), jnp.stack(vp_l), jnp.stack(sp_l), jnp.stack(ks_l), jnp.stack(vs_l), jnp.stack(ss_l))
```

```python
import functools
import math
from typing import NamedTuple

import jax
import jax.numpy as jnp
import numpy as np
from jax import lax
from jax.experimental import pallas as pl
from jax.experimental.pallas import tpu as pltpu

F32 = jnp.float32
BF16 = jnp.bfloat16
LANES = 128
VMEM_LIMIT = 56 * 1024 * 1024


class Cfg(NamedTuple):
    d_model: int = 1024
    chunk: int = 64
    eps: float = 1e-6
    da_heads: int = 8
    da_head_dim: int = 64
    rope_theta: float = 10000.0
    gla_heads: int = 4
    gla_gate_rank: int = 16
    gla_gate_tau: float = 16.0
    peer_heads: int = 8
    peer_n_keys: int = 128
    peer_qdim: int = 256
    peer_topk: int = 16

    @property
    def da_v_dim(self): return 2 * self.da_head_dim
    @property
    def da_w(self): return self.da_heads * 2 * self.da_head_dim
    @property
    def gla_dk(self): return self.d_model // 2 // self.gla_heads
    @property
    def gla_dv(self): return self.d_model // self.gla_heads
    @property
    def gla_qk_w(self): return self.gla_heads * self.gla_dk
    @property
    def gla_v_w(self): return self.gla_heads * self.gla_dv
    @property
    def n_experts(self): return self.peer_n_keys * self.peer_n_keys


def _dot(a, b):
    return jnp.dot(a, b, preferred_element_type=F32)


def _dot_nt(a, b):
    return lax.dot_general(a, b, (((1,), (1,)), ((), ())), preferred_element_type=F32)


def _split_bf16(t):
    hi = t.astype(BF16)
    lo = (t - hi.astype(F32)).astype(BF16)
    return hi, lo


def _params(n_axes, arbitrary_last=False):
    sem = ["parallel"] * n_axes
    if arbitrary_last:
        sem[-1] = "arbitrary"
    return pltpu.CompilerParams(dimension_semantics=tuple(sem), vmem_limit_bytes=VMEM_LIMIT)


def _rope_table_kernel(pos_ref, freq_ref, sign_ref, cos_ref, sin_ref):
    ang = pos_ref[...] * freq_ref[...]
    cos_ref[...] = jnp.cos(ang)
    sin_ref[...] = jnp.sin(ang) * sign_ref[...]


def _rope_tables(cfg, positions):
    half = cfg.da_head_dim // 2
    r = positions.shape[0]
    lane = np.arange(LANES)
    inv_freq = cfg.rope_theta ** (-jnp.arange(half, dtype=F32) / half)
    freq = jnp.tile(inv_freq, LANES // half)[None, :]
    sign = jnp.asarray(np.where(lane % cfg.da_head_dim < half, -1.0, 1.0), F32)[None, :]
    pos = jnp.broadcast_to(positions.astype(F32)[:, None], (r, LANES))
    full = lambda shape: pl.BlockSpec(shape, lambda: (0, 0))
    return pl.pallas_call(
        _rope_table_kernel,
        out_shape=(jax.ShapeDtypeStruct((r, LANES), F32),) * 2,
        in_specs=[full((r, LANES)), full((1, LANES)), full((1, LANES))],
        out_specs=(full((r, LANES)),) * 2,
        name="rope_table",
    )(pos, freq, sign)


def _log_sigmoid(x):
    return jnp.minimum(x, 0.0) - jnp.log1p(jnp.exp(-jnp.abs(x)))


def _in_proj_kernel(cfg, x_ref, g1_ref, w_ref, cos_ref, sin_ref, qkg_ref, seg_ref, segt_ref, w2_ref, b2_ref,
                    q_out, k_out, kb_out, v_out, vb_out, gq_out, gk_out, gv_out, la_out, gr_out, ga_out, gb_out):
    d = cfg.d_model
    qw, gqk, gvw = cfg.da_w, cfg.gla_qk_w, cfg.gla_v_w
    x = x_ref[...]
    xn = (x * lax.rsqrt(jnp.mean(x * x, axis=-1, keepdims=True) + cfg.eps) * g1_ref[...]).astype(BF16)

    off = [0]

    def proj(width):
        y = _dot(xn, w_ref[:, off[0]:off[0] + width])
        off[0] += width
        return y

    nrep = qw // LANES
    cos = jnp.tile(cos_ref[...], (1, nrep))
    sin = jnp.tile(sin_ref[...], (1, nrep))
    lane = lax.broadcasted_iota(jnp.int32, (x.shape[0], qw), 1)
    first_half = (lane & (cfg.da_head_dim - 1)) < (cfg.da_head_dim // 2)
    seg = seg_ref[...]
    segt = segt_ref[...]

    def qk_norm_rope(t, gain):
        hi, lo = _split_bf16(t * t)
        ss = _dot(hi, seg) + _dot(lo, seg)
        rhi, rlo = _split_bf16(lax.rsqrt(ss * (1.0 / cfg.da_head_dim) + cfg.eps))
        y = t * (_dot(rhi, segt) + _dot(rlo, segt)) * gain
        half = cfg.da_head_dim // 2
        partner = jnp.where(first_half, pltpu.roll(y, qw - half, 1), pltpu.roll(y, half, 1))
        return y * cos + partner * sin

    q = qk_norm_rope(proj(qw), qkg_ref[0:1, :])
    q_out[...] = (q * (cfg.da_head_dim ** -0.5)).astype(BF16)
    k = qk_norm_rope(proj(qw), qkg_ref[1:2, :])
    k_out[...] = k
    kb_out[...] = k.astype(BF16)
    v = proj(qw)
    v_out[...] = v
    vb_out[...] = v.astype(BF16)
    gq_out[...] = proj(gqk) * (cfg.gla_dk ** -0.5)
    gk_out[...] = proj(gqk)
    gv_out[...] = proj(gvw)
    gr = proj(gvw)
    gr_out[...] = gr * jax.nn.sigmoid(gr)
    ga_out[...] = jax.nn.sigmoid(proj(d))
    gb_out[...] = jax.nn.sigmoid(proj(d))
    glr = proj(LANES)
    gate = _dot(glr.astype(BF16), w2_ref[...]) + b2_ref[...]
    la_out[...] = _log_sigmoid(gate) * (1.0 / cfg.gla_gate_tau)


def _in_proj(cfg, x, cos, sin, p, tm):
    t, d = x.shape
    qw, gqk, gvw = cfg.da_w, cfg.gla_qk_w, cfg.gla_v_w
    n_tab = cos.shape[0] // tm
    tok = lambda w: pl.BlockSpec((tm, w), lambda i: (i, 0))
    const = lambda a: pl.BlockSpec(a.shape, lambda i: (0,) * a.ndim, pipeline_mode=pl.Buffered(1))
    tab = pl.BlockSpec((tm, LANES), lambda i: (i % n_tab, 0))
    consts = [p["g1"], p["w_all"]]
    consts2 = [p["qk_gain"], p["seg"], p["segt"], p["w_gate2"], p["b_gate"]]
    outs = [(qw, BF16), (qw, F32), (qw, BF16), (qw, F32), (qw, BF16), (gqk, F32), (gqk, F32), (gvw, F32),
            (gqk, F32), (gvw, F32), (d, F32), (d, F32)]
    return pl.pallas_call(
        functools.partial(_in_proj_kernel, cfg),
        grid=(t // tm,),
        in_specs=[tok(d)] + [const(a) for a in consts] + [tab, tab] + [const(a) for a in consts2],
        out_specs=[tok(w) for w, _ in outs],
        out_shape=[jax.ShapeDtypeStruct((t, w), dt) for w, dt in outs],
        compiler_params=_params(1),
        name="in_proj",
    )(x, *consts, cos, sin, *consts2)


def _diff_attn_kernel(cfg, past, tq, tk, lam_init, q_ref, k_ref, v_ref, lamp_ref, g_ref, o_ref,
                      m_ref, l_ref, acc_ref):
    qi = pl.program_id(2)
    hd, vd = cfg.da_head_dim, cfg.da_v_dim
    shift = int(math.log2(cfg.chunk))
    q = q_ref[0]
    lane = lax.broadcasted_iota(jnp.int32, q.shape, 1)
    zero = jnp.zeros_like(q)
    qs = jnp.concatenate([jnp.where(lane < hd, q, zero), jnp.where(lane >= hd, q, zero)], axis=0)

    m_ref[...] = jnp.full(m_ref.shape, -jnp.inf, F32)
    l_ref[...] = jnp.zeros(l_ref.shape, F32)
    acc_ref[...] = jnp.zeros(acc_ref.shape, F32)

    q_start = past + qi * tq
    n_blocks = (q_start + tq + tk - 1) // tk
    n_full = (q_start + cfg.chunk) // tk

    def step(j, masked):
        ks = pl.multiple_of(j * tk, tk)
        s = _dot_nt(qs, k_ref[0, pl.ds(ks, tk), :])
        if masked:
            row = lax.broadcasted_iota(jnp.int32, s.shape, 0)
            col = lax.broadcasted_iota(jnp.int32, s.shape, 1)
            qpos = q_start + jnp.where(row >= tq, row - tq, row)
            allowed = lax.shift_right_logical(ks + col, shift) <= lax.shift_right_logical(qpos, shift)
            s = jnp.where(allowed, s, -jnp.inf)
        m_prev = m_ref[...]
        m_new = jnp.maximum(m_prev, jnp.max(s, axis=-1, keepdims=True))
        p = jnp.exp(s - m_new)
        alpha = jnp.exp(m_prev - m_new)
        l_ref[...] = alpha * l_ref[...] + jnp.sum(p, axis=-1, keepdims=True)
        acc_ref[...] = alpha * acc_ref[...] + _dot(p.astype(BF16), v_ref[0, pl.ds(ks, tk), :])
        m_ref[...] = m_new

    lax.fori_loop(0, n_full, lambda j, c: (step(j, False), c)[1], 0)
    lax.fori_loop(n_full, n_blocks, lambda j, c: (step(j, True), c)[1], 0)

    lp = lamp_ref[...]
    lam = (jnp.exp(jnp.sum(lp[0:1] * lp[1:2], axis=-1, keepdims=True))
           - jnp.exp(jnp.sum(lp[2:3] * lp[3:4], axis=-1, keepdims=True)) + lam_init)
    o = acc_ref[...] / l_ref[...]
    o = o[:tq] - lam * o[tq:]
    o = o * lax.rsqrt(jnp.mean(o * o, axis=-1, keepdims=True) + cfg.eps) * g_ref[...] * (1.0 - lam_init)
    o_ref[0] = o.astype(BF16)


def _diff_attn(cfg, q, k, v, lam_p, subln_g, past, tq, tk, lam_init):
    b, lq, w = q.shape
    lk = k.shape[1]
    vd = cfg.da_v_dim
    assert lq % tq == 0 and lk % tk == 0 and tq % cfg.chunk == 0 and past % cfg.chunk == 0
    assert past + lq == lk
    kv_spec = pl.BlockSpec((1, lk, vd), lambda bi, h, qi: (bi, 0, h))
    q_spec = pl.BlockSpec((1, tq, vd), lambda bi, h, qi: (bi, qi, h))
    return pl.pallas_call(
        functools.partial(_diff_attn_kernel, cfg, past, tq, tk, lam_init),
        grid=(b, cfg.da_heads, lq // tq),
        in_specs=[q_spec, kv_spec, kv_spec,
                  pl.BlockSpec(lam_p.shape, lambda bi, h, qi: (0, 0)),
                  pl.BlockSpec(subln_g.shape, lambda bi, h, qi: (0, 0))],
        out_specs=q_spec,
        out_shape=jax.ShapeDtypeStruct((b, lq, w), BF16),
        scratch_shapes=[pltpu.VMEM((2 * tq, 1), F32), pltpu.VMEM((2 * tq, 1), F32),
                        pltpu.VMEM((2 * tq, vd), F32)],
        compiler_params=_params(3),
        name="diff_attn",
    )(q, k, v, lam_p, subln_g)


def _gla_constants(c):
    r = np.arange(c)
    i = r[:, None]
    rr = r[None, :]
    mats = [(rr <= i), (rr > i)]
    levels = []
    s = c // 2
    while s >= 1:
        levels.append(s)
        s //= 2
    for s in levels:
        bd = (i // s) * s
        mats.append((rr > bd) & (rr <= i))
        bd2 = (i // s + 1) * s
        mats.append((rr > i) & (rr <= bd2))
    stack = np.concatenate([m.astype(np.float32) for m in mats], axis=0)
    lev = np.full((c, c), -1, np.int32)
    jj = r[None, :]
    for n, s in enumerate(levels):
        sel = (jj < i) & (i // s != jj // s) & (lev < 0)
        lev[sel] = n
    lev[i == jj] = len(levels)
    return stack, lev, len(levels)


def _gla_kernel(cfg, n_levels, n_sub, q_ref, k_ref, v_ref, a_ref, s0_ref, mat_ref, lev_ref, g_ref,
                o_ref, s_out_ref, st_ref):
    c = cfg.chunk
    step_i = pl.program_id(2)

    @pl.when(step_i == 0)
    def _():
        st_ref[...] = s0_ref[0, 0].T

    mat = mat_ref[...]
    lev = lev_ref[...]

    def chunk(n, carry):
        rows = pl.ds(pl.multiple_of(n * c, c), c)
        q, k, v, a = q_ref[0, rows, :], k_ref[0, rows, :], v_ref[0, rows, :], a_ref[0, rows, :]
        a_hi, a_lo = _split_bf16(a)
        e = _dot(mat, a_hi) + _dot(mat, a_lo)
        b = e[0:c]
        to_last = e[c:2 * c]
        att = jnp.where(lev == n_levels, _dot_nt(q.astype(BF16), k.astype(BF16)), 0.0)
        for n_l in range(n_levels):
            eq = e[(2 + 2 * n_l) * c:(3 + 2 * n_l) * c]
            ek = e[(3 + 2 * n_l) * c:(4 + 2 * n_l) * c]
            a_l = _dot_nt((q * jnp.exp(eq)).astype(BF16), (k * jnp.exp(ek)).astype(BF16))
            att = jnp.where(lev == n_l, a_l, att)
        st = st_ref[...]
        vb = v.astype(BF16)
        o = _dot_nt((q * jnp.exp(b)).astype(BF16), st.astype(BF16)) + _dot(att.astype(BF16), vb)
        kl = (k * jnp.exp(to_last)).astype(BF16)
        upd = lax.dot_general(vb, kl, (((0,), (0,)), ((), ())), preferred_element_type=F32)
        st_ref[...] = st * jnp.exp(b[c - 1:c, :]) + upd
        o = o * lax.rsqrt(jnp.mean(o * o, axis=-1, keepdims=True) + cfg.eps) * g_ref[...]
        o_ref[0, rows, :] = o
        return carry

    lax.fori_loop(0, n_sub, chunk, 0)

    @pl.when(step_i == pl.num_programs(2) - 1)
    def _():
        s_out_ref[0, 0] = st_ref[...].T


def _gla(cfg, gq, gk, gv, la, s0, norm_g, lb):
    b, l, _ = gq.shape
    c, dk, dv, h = cfg.chunk, cfg.gla_dk, cfg.gla_dv, cfg.gla_heads
    assert l % lb == 0 and lb % c == 0
    stack, lev, n_levels = _gla_constants(c)
    mat = jnp.asarray(stack, BF16)
    lev = jnp.asarray(lev)
    seq = lambda w: pl.BlockSpec((1, lb, w), lambda bi, hi, si: (bi, si, hi))
    state = pl.BlockSpec((1, 1, dk, dv), lambda bi, hi, si: (bi, hi, 0, 0))
    const = lambda a: pl.BlockSpec(a.shape, lambda bi, hi, si: (0,) * a.ndim)
    return pl.pallas_call(
        functools.partial(_gla_kernel, cfg, n_levels, lb // c),
        grid=(b, h, l // lb),
        in_specs=[seq(dk), seq(dk), seq(dv), seq(dk), state, const(mat), const(lev), const(norm_g)],
        out_specs=[seq(dv), state],
        out_shape=[jax.ShapeDtypeStruct((b, l, h * dv), F32), jax.ShapeDtypeStruct((b, h, dk, dv), F32)],
        scratch_shapes=[pltpu.VMEM((dv, dk), F32)],
        compiler_params=_params(3, arbitrary_last=True),
        name="gla",
    )(gq, gk, gv, la, s0, mat, lev, norm_g)


def _merge_kernel(cfg, x_ref, oa_ref, ob_ref, gr_ref, ga_ref, gb_ref, wa_ref, wb_ref, wo_ref, g2_ref,
                  h_ref, hn_ref):
    a = _dot(oa_ref[...], wa_ref[...])
    bb = _dot((ob_ref[...] * gr_ref[...]).astype(BF16), wb_ref[...])
    mix = ga_ref[...] * a + gb_ref[...] * bb
    h = x_ref[...] + _dot(mix.astype(BF16), wo_ref[...])
    h_ref[...] = h
    hn = h * lax.rsqrt(jnp.mean(h * h, axis=-1, keepdims=True) + cfg.eps) * g2_ref[...]
    hn_ref[...] = hn.astype(BF16)


def _merge_out(cfg, x, oa, ob, gr, ga, gb, p, tm):
    t, d = x.shape
    tok = lambda a: pl.BlockSpec((tm, a.shape[1]), lambda i: (i, 0))
    const = lambda a: pl.BlockSpec(a.shape, lambda i: (0,) * a.ndim, pipeline_mode=pl.Buffered(1))
    toks = [x, oa, ob, gr, ga, gb]
    consts = [p["w_a"], p["w_b"], p["w_o"], p["g2"]]
    return pl.pallas_call(
        functools.partial(_merge_kernel, cfg),
        grid=(t // tm,),
        in_specs=[tok(a) for a in toks] + [const(a) for a in consts],
        out_specs=[pl.BlockSpec((tm, d), lambda i: (i, 0))] * 2,
        out_shape=[jax.ShapeDtypeStruct((t, d), F32), jax.ShapeDtypeStruct((t, d), BF16)],
        compiler_params=_params(1),
        name="merge_out",
    )(*toks, *consts)


def _top_values(x, k):
    vals = []
    for _ in range(k):
        m = jnp.max(x, axis=0, keepdims=True)
        vals.append(m)
        x = jnp.where(x == m, -jnp.inf, x)
    return vals


def _rows_to_block(rows):
    k, t = len(rows), rows[0].shape[1]
    idx = lax.broadcasted_iota(jnp.int32, (k, t), 0)
    out = jnp.zeros((k, t), F32)
    for n, r in enumerate(rows):
        out = jnp.where(idx == n, r, out)
    return out


def _peer_route_kernel(cfg, hn_ref, wq_ref, keys_ref, s1_ref, s2_ref, e1_ref, e2_ref, tau_ref):
    nk, topk = cfg.peer_n_keys, cfg.peer_topk
    sub = cfg.peer_qdim // 2
    q = _dot(hn_ref[...], wq_ref[...]).astype(BF16)
    taus = []
    for h in range(cfg.peer_heads):
        sc, top = [], []
        for half in range(2):
            col = (2 * h + half) * sub
            s = _dot_nt(keys_ref[2 * h + half], q[:, col:col + sub])
            sc.append(s)
            top.append(_top_values(s, topk))
        sv2 = _rows_to_block(top[1])
        cand = jnp.concatenate([r + sv2 for r in top[0]], axis=0)
        best = _top_values(cand, topk)
        z = best[0] * 0.0
        for cv in best:
            z = z + jnp.exp(cv - best[0])
        rows = pl.ds(h * nk, nk)
        s1_ref[rows, :] = sc[0]
        s2_ref[rows, :] = sc[1]
        e1_ref[rows, :] = jnp.exp(sc[0] - top[0][0]) / z
        e2_ref[rows, :] = jnp.exp(sc[1] - top[1][0])
        taus.append(best[topk - 1])
    tau_ref[...] = _rows_to_block(taus)


def _peer_route(cfg, hn, wq, keys, tb):
    t, d = hn.shape
    hk = cfg.peer_heads * cfg.peer_n_keys
    const = lambda a: pl.BlockSpec(a.shape, lambda i: (0,) * a.ndim, pipeline_mode=pl.Buffered(1))
    col = lambda r: pl.BlockSpec((r, tb), lambda i: (0, i))
    return pl.pallas_call(
        functools.partial(_peer_route_kernel, cfg),
        grid=(t // tb,),
        in_specs=[pl.BlockSpec((tb, d), lambda i: (i, 0)), const(wq), const(keys)],
        out_specs=[col(hk)] * 4 + [col(cfg.peer_heads)],
        out_shape=[jax.ShapeDtypeStruct((hk, t), F32)] * 4 + [jax.ShapeDtypeStruct((cfg.peer_heads, t), F32)],
        compiler_params=_params(1),
        name="peer_route",
    )(hn, wq, keys)


def _gelu(x):
    return 0.5 * x * (1.0 + lax.erf(x * (2.0 ** -0.5)))


def _peer_dense_kernel(cfg, ib, hn_ref, h_ref, u_ref, vt_ref, s1_ref, s2_ref, e1_ref, e2_ref, tau_ref,
                       y_ref, acc_ref, p_ref):
    nk = cfg.peer_n_keys
    ei = pl.program_id(1)
    tb = hn_ref.shape[0]

    @pl.when(ei == 0)
    def _():
        acc_ref[...] = jnp.zeros(acc_ref.shape, F32)

    act = _gelu(_dot_nt(u_ref[...], hn_ref[...]))

    def gate_rows(tc, carry):
        lanes = pl.ds(pl.multiple_of(tc * LANES, LANES), LANES)
        for i_loc in range(ib):
            g = jnp.zeros((nk, LANES), F32)
            for h in range(cfg.peer_heads):
                s1 = s1_ref[h, i_loc:i_loc + 1, lanes]
                e1 = e1_ref[h, i_loc:i_loc + 1, lanes]
                keep = (s1 + s2_ref[h * nk:(h + 1) * nk, lanes]) >= tau_ref[h:h + 1, lanes]
                g = g + jnp.where(keep, e2_ref[h * nk:(h + 1) * nk, lanes], 0.0) * e1
            p_ref[i_loc * nk:(i_loc + 1) * nk, lanes] = g
        return carry

    lax.fori_loop(0, tb // LANES, gate_rows, 0)
    acc_ref[...] += _dot(vt_ref[...], (p_ref[...] * act).astype(BF16))

    @pl.when(ei == pl.num_programs(1) - 1)
    def _():
        y_ref[...] = h_ref[...] + acc_ref[...].T


def _peer_dense(cfg, hn, h, u, vt, route, tb, ib):
    t, d = hn.shape
    nk = cfg.peer_n_keys
    eb = ib * nk
    hk = cfg.peer_heads * nk
    s1, s2, e1, e2, tau = route
    by_head = lambda a: a.reshape(cfg.peer_heads, nk, t)
    tokb = pl.BlockSpec((tb, d), lambda ti, ei: (ti, 0))
    col = lambda r: pl.BlockSpec((r, tb), lambda ti, ei: (0, ti))
    first = pl.BlockSpec((cfg.peer_heads, ib, tb), lambda ti, ei: (0, ei, ti))
    return pl.pallas_call(
        functools.partial(_peer_dense_kernel, cfg, ib),
        grid=(t // tb, cfg.n_experts // eb),
        in_specs=[tokb, tokb,
                  pl.BlockSpec((eb, d), lambda ti, ei: (ei, 0)),
                  pl.BlockSpec((d, eb), lambda ti, ei: (0, ei)),
                  first, col(hk), first, col(hk), col(cfg.peer_heads)],
        out_specs=tokb,
        out_shape=jax.ShapeDtypeStruct((t, d), F32),
        scratch_shapes=[pltpu.VMEM((d, tb), F32), pltpu.VMEM((eb, tb), F32)],
        compiler_params=_params(2, arbitrary_last=True),
        name="peer_dense",
    )(hn, h, u, vt, by_head(s1), s2, by_head(e1), e2, tau)


def _prep_params(cfg, norm1_g, w_in, qk_norm_g, da_lambda, da_subln_g, gla_w_gate2, gla_b_gate, gla_norm_g,
                 w_branch_a, w_branch_b, w_out, norm2_g, peer_w_q, peer_sub_keys, peer_u, peer_v):
    d, qw, gqk, gvw, rank = cfg.d_model, cfg.da_w, cfg.gla_qk_w, cfg.gla_v_w, cfg.gla_gate_rank
    splits = (qw, qw, qw, gqk, gqk, gvw, rank, gvw, d, d)
    starts = np.concatenate([[0], np.cumsum(splits)])
    piece = lambda n: w_in[:, starts[n]:starts[n + 1]]
    glr = jnp.pad(piece(6), ((0, 0), (0, LANES - rank)))
    w_all = jnp.concatenate([piece(n) for n in (0, 1, 2, 3, 4, 5, 7, 8, 9)] + [glr], axis=1).astype(BF16)
    group = np.arange(qw) // cfg.da_head_dim
    seg = (group[:, None] == np.arange(LANES)[None, :]).astype(np.float32)
    return {
        "g1": norm1_g[None, :], "w_all": w_all,
        "qk_gain": jnp.tile(qk_norm_g, (1, qw // cfg.da_head_dim)),
        "seg": jnp.asarray(seg, BF16), "segt": jnp.asarray(seg.T, BF16),
        "w_gate2": jnp.pad(gla_w_gate2, ((0, LANES - rank), (0, 0))).astype(BF16),
        "b_gate": gla_b_gate[None, :],
        "da_lambda": da_lambda, "subln_g": da_subln_g[None, :], "gla_norm_g": gla_norm_g[None, :],
        "w_a": w_branch_a.astype(BF16), "w_b": w_branch_b.astype(BF16), "w_o": w_out.astype(BF16),
        "g2": norm2_g[None, :],
        "wq": peer_w_q.astype(BF16),
        "keys": peer_sub_keys.reshape(cfg.peer_heads * 2, cfg.peer_n_keys, cfg.peer_qdim // 2).astype(BF16),
        "u": peer_u.astype(BF16), "vt": peer_v.astype(BF16).T,
    }


def _pick(n, target):
    t = min(n, target)
    while n % t:
        t //= 2
    return t


def _layer(cfg, x, cache_k, cache_v, s0, p, lam_init):
    b, l, d = x.shape
    t = b * l
    past = 0 if cache_k is None else cache_k.shape[1]
    qw = cfg.da_w

    tm = _pick(t, 256)
    if l % tm == 0:
        positions = past + jnp.arange(l)
    else:
        positions = past + (jnp.arange(tm) % l)
    cos, sin = _rope_tables(cfg, positions)
    q, k, kb, v, vb, gq, gk, gv, la, gr, ga, gb = _in_proj(cfg, x.reshape(t, d), cos, sin, p, tm)

    r3 = lambda a: a.reshape(b, l, a.shape[-1])
    if cache_k is None:
        kk, vv = r3(kb), r3(vb)
        tq = _pick(l, 256)
        tk = tq
    else:
        kk = jnp.concatenate([cache_k.reshape(b, past, qw).astype(BF16), r3(kb)], axis=1)
        vv = jnp.concatenate([cache_v.reshape(b, past, qw).astype(BF16), r3(vb)], axis=1)
        tq = _pick(l, 256)
        tk = past + l
    oa = _diff_attn(cfg, r3(q), kk, vv, p["da_lambda"], p["subln_g"], past, tq, tk, lam_init)

    if s0 is None:
        s0 = jnp.zeros((b, cfg.gla_heads, cfg.gla_dk, cfg.gla_dv), F32)
    ob, s_new = _gla(cfg, r3(gq), r3(gk), r3(gv), r3(la), s0, p["gla_norm_g"], _pick(l, 512))

    h, hn = _merge_out(cfg, x.reshape(t, d), oa.reshape(t, qw), ob.reshape(t, -1), gr, ga, gb, p, tm)

    tb = _pick(t, 512)
    route = _peer_route(cfg, hn, p["wq"], p["keys"], _pick(t, 256))
    y = _peer_dense(cfg, hn, h, p["u"], p["vt"], route, tb, 8)

    k_new = k.reshape(b, l, cfg.da_heads, 2, cfg.da_head_dim)
    v_new = v.reshape(b, l, cfg.da_heads, cfg.da_v_dim)
    return y.reshape(b, l, d), k_new, v_new, s_new


def _forward(cfg, x_prompt, x_sample, cache_k, cache_v, state_gla, *weights):
    yp, ys = x_prompt, x_sample
    outs = [[] for _ in range(6)]
    for layer in range(cache_k.shape[0]):
        p = _prep_params(cfg, *[w[layer] for w in weights])
        lam_init = 0.8 - 0.6 * math.exp(-0.3 * layer)
        yp, kp, vp, sp = _layer(cfg, yp, None, None, None, p, lam_init)
        ys, kn, vn, sn = _layer(cfg, ys, cache_k[layer], cache_v[layer], state_gla[layer], p, lam_init)
        for lst, a in zip(outs, (kp, vp, sp, kn, vn, sn)):
            lst.append(a)
    return (yp, ys) + tuple(jnp.stack(o) for o in outs)


def kernel(x_prompt, x_sample, cache_k, cache_v, state_gla, norm1_g, w_in, qk_norm_g, da_lambda, da_subln_g,
           gla_w_gate2, gla_b_gate, gla_norm_g, w_branch_a, w_branch_b, w_out, norm2_g, peer_w_q,
           peer_sub_keys, peer_u, peer_v):
    return _forward(Cfg(), x_prompt, x_sample, cache_k, cache_v, state_gla, norm1_g, w_in, qk_norm_g, da_lambda,
                    da_subln_g, gla_w_gate2, gla_b_gate, gla_norm_g, w_branch_a, w_branch_b, w_out, norm2_g,
                    peer_w_q, peer_sub_keys, peer_u, peer_v)
```

```python
import functools
import math
from typing import NamedTuple

import jax
import jax.numpy as jnp
import numpy as np
from jax import lax
from jax.experimental import pallas as pl
from jax.experimental.pallas import tpu as pltpu

F32 = jnp.float32
BF16 = jnp.bfloat16
LANES = 128
VMEM_LIMIT = 56 * 1024 * 1024
GATE_ROWS = 16


class Cfg(NamedTuple):
    d_model: int = 1024
    chunk: int = 64
    eps: float = 1e-6
    da_heads: int = 8
    da_head_dim: int = 64
    rope_theta: float = 10000.0
    gla_heads: int = 4
    gla_gate_rank: int = 16
    gla_gate_tau: float = 16.0
    peer_heads: int = 8
    peer_n_keys: int = 128
    peer_qdim: int = 256
    peer_topk: int = 16

    @property
    def da_v_dim(self): return 2 * self.da_head_dim
    @property
    def da_w(self): return self.da_heads * 2 * self.da_head_dim
    @property
    def gla_dk(self): return self.d_model // 2 // self.gla_heads
    @property
    def gla_dv(self): return self.d_model // self.gla_heads
    @property
    def gla_qk_w(self): return self.gla_heads * self.gla_dk
    @property
    def gla_v_w(self): return self.gla_heads * self.gla_dv
    @property
    def n_experts(self): return self.peer_n_keys * self.peer_n_keys


def _dot(a, b):
    return jnp.dot(a, b, preferred_element_type=F32)


def _dot_nt(a, b):
    return lax.dot_general(a, b, (((1,), (1,)), ((), ())), preferred_element_type=F32)


def _split_bf16(t):
    hi = t.astype(BF16)
    lo = (t - hi.astype(F32)).astype(BF16)
    return hi, lo


def _params(n_axes, arbitrary_last=False):
    sem = ["parallel"] * n_axes
    if arbitrary_last:
        sem[-1] = "arbitrary"
    return pltpu.CompilerParams(dimension_semantics=tuple(sem), vmem_limit_bytes=VMEM_LIMIT)


def _rope_table_kernel(pos_ref, freq_ref, sign_ref, cos_ref, sin_ref):
    ang = pos_ref[...] * freq_ref[...]
    cos_ref[...] = jnp.cos(ang)
    sin_ref[...] = jnp.sin(ang) * sign_ref[...]


def _rope_tables(cfg, positions):
    half = cfg.da_head_dim // 2
    r = positions.shape[0]
    lane = np.arange(LANES)
    inv_freq = cfg.rope_theta ** (-jnp.arange(half, dtype=F32) / half)
    freq = jnp.tile(inv_freq, LANES // half)[None, :]
    sign = jnp.asarray(np.where(lane % cfg.da_head_dim < half, -1.0, 1.0), F32)[None, :]
    pos = jnp.broadcast_to(positions.astype(F32)[:, None], (r, LANES))
    full = lambda shape: pl.BlockSpec(shape, lambda: (0, 0))
    return pl.pallas_call(
        _rope_table_kernel,
        out_shape=(jax.ShapeDtypeStruct((r, LANES), F32),) * 2,
        in_specs=[full((r, LANES)), full((1, LANES)), full((1, LANES))],
        out_specs=(full((r, LANES)),) * 2,
        name="rope_table",
    )(pos, freq, sign)


def _log_sigmoid(x):
    return jnp.minimum(x, 0.0) - jnp.log1p(jnp.exp(-jnp.abs(x)))


def _in_proj_kernel(cfg, x_ref, g1_ref, w_ref, cos_ref, sin_ref, qkg_ref, seg_ref, segt_ref, w2_ref, b2_ref,
                    q_out, k_out, kb_out, v_out, vb_out, gq_out, gk_out, gv_out, la_out, gr_out, ga_out, gb_out):
    d = cfg.d_model
    qw, gqk, gvw = cfg.da_w, cfg.gla_qk_w, cfg.gla_v_w
    x = x_ref[...]
    xn = (x * lax.rsqrt(jnp.mean(x * x, axis=-1, keepdims=True) + cfg.eps) * g1_ref[...]).astype(BF16)

    off = [0]

    def proj(width):
        y = _dot(xn, w_ref[:, off[0]:off[0] + width])
        off[0] += width
        return y

    nrep = qw // LANES
    cos = jnp.tile(cos_ref[...], (1, nrep))
    sin = jnp.tile(sin_ref[...], (1, nrep))
    lane = lax.broadcasted_iota(jnp.int32, (x.shape[0], qw), 1)
    first_half = (lane & (cfg.da_head_dim - 1)) < (cfg.da_head_dim // 2)
    seg = seg_ref[...]
    segt = segt_ref[...]

    def qk_norm_rope(t, gain):
        hi, lo = _split_bf16(t * t)
        ss = _dot(hi, seg) + _dot(lo, seg)
        rhi, rlo = _split_bf16(lax.rsqrt(ss * (1.0 / cfg.da_head_dim) + cfg.eps))
        y = t * (_dot(rhi, segt) + _dot(rlo, segt)) * gain
        half = cfg.da_head_dim // 2
        partner = jnp.where(first_half, pltpu.roll(y, qw - half, 1), pltpu.roll(y, half, 1))
        return y * cos + partner * sin

    q = qk_norm_rope(proj(qw), qkg_ref[0:1, :])
    q_out[...] = (q * (cfg.da_head_dim ** -0.5)).astype(BF16)
    k = qk_norm_rope(proj(qw), qkg_ref[1:2, :])
    k_out[...] = k
    kb_out[...] = k.astype(BF16)
    v = proj(qw)
    v_out[...] = v
    vb_out[...] = v.astype(BF16)
    gq_out[...] = proj(gqk) * (cfg.gla_dk ** -0.5)
    gk_out[...] = proj(gqk)
    gv_out[...] = proj(gvw)
    gr = proj(gvw)
    gr_out[...] = gr * jax.nn.sigmoid(gr)
    ga_out[...] = jax.nn.sigmoid(proj(d))
    gb_out[...] = jax.nn.sigmoid(proj(d))
    glr = proj(LANES)
    gate = _dot(glr.astype(BF16), w2_ref[...]) + b2_ref[...]
    la_out[...] = _log_sigmoid(gate) * (1.0 / cfg.gla_gate_tau)


def _in_proj(cfg, x, cos, sin, p, tm):
    t, d = x.shape
    qw, gqk, gvw = cfg.da_w, cfg.gla_qk_w, cfg.gla_v_w
    n_tab = cos.shape[0] // tm
    tok = lambda w: pl.BlockSpec((tm, w), lambda i: (i, 0))
    const = lambda a: pl.BlockSpec(a.shape, lambda i: (0,) * a.ndim, pipeline_mode=pl.Buffered(1))
    tab = pl.BlockSpec((tm, LANES), lambda i: (i % n_tab, 0))
    consts = [p["g1"], p["w_all"]]
    consts2 = [p["qk_gain"], p["seg"], p["segt"], p["w_gate2"], p["b_gate"]]
    outs = [(qw, BF16), (qw, F32), (qw, BF16), (qw, F32), (qw, BF16), (gqk, F32), (gqk, F32), (gvw, F32),
            (gqk, F32), (gvw, F32), (d, F32), (d, F32)]
    return pl.pallas_call(
        functools.partial(_in_proj_kernel, cfg),
        grid=(t // tm,),
        in_specs=[tok(d)] + [const(a) for a in consts] + [tab, tab] + [const(a) for a in consts2],
        out_specs=[tok(w) for w, _ in outs],
        out_shape=[jax.ShapeDtypeStruct((t, w), dt) for w, dt in outs],
        compiler_params=_params(1),
        name="in_proj",
    )(x, *consts, cos, sin, *consts2)


def _diff_attn_kernel(cfg, past, tq, tk, lam_init, q_ref, k_ref, v_ref, lamp_ref, g_ref, o_ref,
                      s_ref, mpart_ref, mrow_ref, lpart_ref, acc_ref):
    qi = pl.program_id(2)
    hd, vd = cfg.da_head_dim, cfg.da_v_dim
    shift = int(math.log2(cfg.chunk))
    n_lane_chunks = tk // LANES
    q = q_ref[0]
    lane = lax.broadcasted_iota(jnp.int32, q.shape, 1)
    zero = jnp.zeros_like(q)
    qs = jnp.concatenate([jnp.where(lane < hd, q, zero), jnp.where(lane >= hd, q, zero)], axis=0)

    q_start = past + qi * tq
    n_blocks = (q_start + tq + tk - 1) // tk
    n_full = (q_start + cfg.chunk) // tk

    mpart_ref[...] = jnp.full(mpart_ref.shape, -jnp.inf, F32)

    def scores(j, masked):
        ks = pl.multiple_of(j * tk, tk)
        s = _dot_nt(qs, k_ref[0, pl.ds(ks, tk), :])
        if masked:
            row = lax.broadcasted_iota(jnp.int32, s.shape, 0)
            col = lax.broadcasted_iota(jnp.int32, s.shape, 1)
            qpos = q_start + jnp.where(row >= tq, row - tq, row)
            allowed = lax.shift_right_logical(ks + col, shift) <= lax.shift_right_logical(qpos, shift)
            s = jnp.where(allowed, s, -jnp.inf)
        s_ref[j] = s
        part = s[:, 0:LANES]
        for c in range(1, n_lane_chunks):
            part = jnp.maximum(part, s[:, c * LANES:(c + 1) * LANES])
        mpart_ref[...] = jnp.maximum(mpart_ref[...], part)

    lax.fori_loop(0, n_full, lambda j, c: (scores(j, False), c)[1], 0)
    lax.fori_loop(n_full, n_blocks, lambda j, c: (scores(j, True), c)[1], 0)

    mrow_ref[...] = jnp.broadcast_to(jnp.max(mpart_ref[...], axis=-1, keepdims=True), mrow_ref.shape)
    lpart_ref[...] = jnp.zeros(lpart_ref.shape, F32)
    acc_ref[...] = jnp.zeros(acc_ref.shape, F32)

    def weigh(j, carry):
        ks = pl.multiple_of(j * tk, tk)
        m = mrow_ref[...]
        ps = [jnp.exp(s_ref[j, :, c * LANES:(c + 1) * LANES] - m) for c in range(n_lane_chunks)]
        lsum = ps[0]
        for pc in ps[1:]:
            lsum = lsum + pc
        lpart_ref[...] += lsum
        p = jnp.concatenate([pc.astype(BF16) for pc in ps], axis=1)
        acc_ref[...] += _dot(p, v_ref[0, pl.ds(ks, tk), :])
        return carry

    lax.fori_loop(0, n_blocks, weigh, 0)

    lp = lamp_ref[...]
    lam = (jnp.exp(jnp.sum(lp[0:1] * lp[1:2], axis=-1, keepdims=True))
           - jnp.exp(jnp.sum(lp[2:3] * lp[3:4], axis=-1, keepdims=True)) + lam_init)
    o = acc_ref[...] / jnp.sum(lpart_ref[...], axis=-1, keepdims=True)
    o = o[:tq] - lam * o[tq:]
    o = o * lax.rsqrt(jnp.mean(o * o, axis=-1, keepdims=True) + cfg.eps) * g_ref[...] * (1.0 - lam_init)
    o_ref[0] = o.astype(BF16)


def _diff_attn(cfg, q, k, v, lam_p, subln_g, past, tq, tk, lam_init):
    b, lq, w = q.shape
    lk = k.shape[1]
    vd = cfg.da_v_dim
    assert lq % tq == 0 and lk % tk == 0 and tk % LANES == 0 and tq % cfg.chunk == 0 and past % cfg.chunk == 0
    assert past + lq <= lk
    kv_spec = pl.BlockSpec((1, lk, vd), lambda bi, h, qi: (bi, 0, h))
    q_spec = pl.BlockSpec((1, tq, vd), lambda bi, h, qi: (bi, qi, h))
    return pl.pallas_call(
        functools.partial(_diff_attn_kernel, cfg, past, tq, tk, lam_init),
        grid=(b, cfg.da_heads, lq // tq),
        in_specs=[q_spec, kv_spec, kv_spec,
                  pl.BlockSpec(lam_p.shape, lambda bi, h, qi: (0, 0)),
                  pl.BlockSpec(subln_g.shape, lambda bi, h, qi: (0, 0))],
        out_specs=q_spec,
        out_shape=jax.ShapeDtypeStruct((b, lq, w), BF16),
        scratch_shapes=[pltpu.VMEM((lk // tk, 2 * tq, tk), F32), pltpu.VMEM((2 * tq, LANES), F32),
                        pltpu.VMEM((2 * tq, LANES), F32), pltpu.VMEM((2 * tq, LANES), F32),
                        pltpu.VMEM((2 * tq, vd), F32)],
        compiler_params=_params(3),
        name="diff_attn",
    )(q, k, v, lam_p, subln_g)


def _gla_constants(c):
    r = np.arange(c)
    i = r[:, None]
    rr = r[None, :]
    mats = [(rr <= i), (rr > i)]
    levels = []
    s = c // 2
    while s >= 1:
        levels.append(s)
        s //= 2
    for s in levels:
        bd = (i // s) * s
        mats.append((rr > bd) & (rr <= i))
        bd2 = (i // s + 1) * s
        mats.append((rr > i) & (rr <= bd2))
    stack = np.concatenate([m.astype(np.float32) for m in mats], axis=0)
    lev = np.full((c, c), -1, np.int32)
    jj = r[None, :]
    for n, s in enumerate(levels):
        sel = (jj < i) & (i // s != jj // s) & (lev < 0)
        lev[sel] = n
    lev[i == jj] = len(levels)
    return stack, lev, len(levels)


def _gla_kernel(cfg, n_levels, n_sub, q_ref, k_ref, v_ref, a_ref, s0_ref, mat_ref, lev_ref, g_ref,
                o_ref, s_out_ref, st_ref):
    c = cfg.chunk
    step_i = pl.program_id(2)

    @pl.when(step_i == 0)
    def _():
        st_ref[...] = s0_ref[0, 0].T

    mat = mat_ref[...]
    lev = lev_ref[...]

    def chunk(n, carry):
        rows = pl.ds(pl.multiple_of(n * c, c), c)
        q, k, v, a = q_ref[0, rows, :], k_ref[0, rows, :], v_ref[0, rows, :], a_ref[0, rows, :]
        a_hi, a_lo = _split_bf16(a)
        e = _dot(mat, a_hi) + _dot(mat, a_lo)
        b = e[0:c]
        to_last = e[c:2 * c]
        att = jnp.where(lev == n_levels, _dot_nt(q.astype(BF16), k.astype(BF16)), 0.0)
        for n_l in range(n_levels):
            eq = e[(2 + 2 * n_l) * c:(3 + 2 * n_l) * c]
            ek = e[(3 + 2 * n_l) * c:(4 + 2 * n_l) * c]
            a_l = _dot_nt((q * jnp.exp(eq)).astype(BF16), (k * jnp.exp(ek)).astype(BF16))
            att = jnp.where(lev == n_l, a_l, att)
        st = st_ref[...]
        vb = v.astype(BF16)
        o = _dot_nt((q * jnp.exp(b)).astype(BF16), st.astype(BF16)) + _dot(att.astype(BF16), vb)
        kl = (k * jnp.exp(to_last)).astype(BF16)
        upd = lax.dot_general(vb, kl, (((0,), (0,)), ((), ())), preferred_element_type=F32)
        st_ref[...] = st * jnp.exp(b[c - 1:c, :]) + upd
        o = o * lax.rsqrt(jnp.mean(o * o, axis=-1, keepdims=True) + cfg.eps) * g_ref[...]
        o_ref[0, rows, :] = o
        return carry

    lax.fori_loop(0, n_sub, chunk, 0)

    @pl.when(step_i == pl.num_programs(2) - 1)
    def _():
        s_out_ref[0, 0] = st_ref[...].T


def _gla(cfg, gq, gk, gv, la, s0, norm_g, lb):
    b, l, _ = gq.shape
    c, dk, dv, h = cfg.chunk, cfg.gla_dk, cfg.gla_dv, cfg.gla_heads
    assert l % lb == 0 and lb % c == 0
    stack, lev, n_levels = _gla_constants(c)
    mat = jnp.asarray(stack, BF16)
    lev = jnp.asarray(lev)
    seq = lambda w: pl.BlockSpec((1, lb, w), lambda bi, hi, si: (bi, si, hi))
    state = pl.BlockSpec((1, 1, dk, dv), lambda bi, hi, si: (bi, hi, 0, 0))
    const = lambda a: pl.BlockSpec(a.shape, lambda bi, hi, si: (0,) * a.ndim)
    return pl.pallas_call(
        functools.partial(_gla_kernel, cfg, n_levels, lb // c),
        grid=(b, h, l // lb),
        in_specs=[seq(dk), seq(dk), seq(dv), seq(dk), state, const(mat), const(lev), const(norm_g)],
        out_specs=[seq(dv), state],
        out_shape=[jax.ShapeDtypeStruct((b, l, h * dv), F32), jax.ShapeDtypeStruct((b, h, dk, dv), F32)],
        scratch_shapes=[pltpu.VMEM((dv, dk), F32)],
        compiler_params=_params(3, arbitrary_last=True),
        name="gla",
    )(gq, gk, gv, la, s0, mat, lev, norm_g)


def _merge_kernel(cfg, x_ref, oa_ref, ob_ref, gr_ref, ga_ref, gb_ref, wa_ref, wb_ref, wo_ref, g2_ref,
                  h_ref, hn_ref):
    a = _dot(oa_ref[...], wa_ref[...])
    bb = _dot((ob_ref[...] * gr_ref[...]).astype(BF16), wb_ref[...])
    mix = ga_ref[...] * a + gb_ref[...] * bb
    h = x_ref[...] + _dot(mix.astype(BF16), wo_ref[...])
    h_ref[...] = h
    hn = h * lax.rsqrt(jnp.mean(h * h, axis=-1, keepdims=True) + cfg.eps) * g2_ref[...]
    hn_ref[...] = hn.astype(BF16)


def _merge_out(cfg, x, oa, ob, gr, ga, gb, p, tm):
    t, d = x.shape
    tok = lambda a: pl.BlockSpec((tm, a.shape[1]), lambda i: (i, 0))
    const = lambda a: pl.BlockSpec(a.shape, lambda i: (0,) * a.ndim, pipeline_mode=pl.Buffered(1))
    toks = [x, oa, ob, gr, ga, gb]
    consts = [p["w_a"], p["w_b"], p["w_o"], p["g2"]]
    return pl.pallas_call(
        functools.partial(_merge_kernel, cfg),
        grid=(t // tm,),
        in_specs=[tok(a) for a in toks] + [const(a) for a in consts],
        out_specs=[pl.BlockSpec((tm, d), lambda i: (i, 0))] * 2,
        out_shape=[jax.ShapeDtypeStruct((t, d), F32), jax.ShapeDtypeStruct((t, d), BF16)],
        compiler_params=_params(1),
        name="merge_out",
    )(*toks, *consts)


def _top_values(x, k):
    vals = []
    rank = jnp.full(x.shape, float(k), F32)
    for n in range(k):
        m = jnp.max(x, axis=0, keepdims=True)
        vals.append(m)
        hit = x == m
        rank = jnp.where(hit, float(n), rank)
        x = jnp.where(hit, -jnp.inf, x)
    return vals, rank


def _rows_to_block(rows):
    k, t = len(rows), rows[0].shape[1]
    idx = lax.broadcasted_iota(jnp.int32, (k, t), 0)
    out = jnp.zeros((k, t), F32)
    for n, r in enumerate(rows):
        out = jnp.where(idx == n, r, out)
    return out


def _peer_route_kernel(cfg, hn_ref, wq_ref, keys_ref, c1_ref, r2_ref, e1_ref, e2_ref):
    nk, topk = cfg.peer_n_keys, cfg.peer_topk
    sub = cfg.peer_qdim // 2
    q = _dot(hn_ref[...], wq_ref[...]).astype(BF16)
    for h in range(cfg.peer_heads):
        sc, top, rank = [], [], []
        for half in range(2):
            col = (2 * h + half) * sub
            s = _dot_nt(keys_ref[2 * h + half], q[:, col:col + sub])
            vals, rk = _top_values(s, topk)
            sc.append(s)
            top.append(vals)
            rank.append(rk)
        sv2 = _rows_to_block(top[1])
        cand = jnp.concatenate([r + sv2 for r in top[0]], axis=0)
        best, _ = _top_values(cand, topk)
        tau = best[topk - 1]
        z = best[0] * 0.0
        for cv in best:
            z = z + jnp.exp(cv - best[0])
        count = jnp.zeros(sc[0].shape, F32)
        for b in range(topk):
            count = count + jnp.where(sc[0] + top[1][b] >= tau, 1.0, 0.0)
        rows = pl.ds(h * nk, nk)
        c1_ref[rows, :] = count
        r2_ref[rows, :] = rank[1]
        e1_ref[rows, :] = jnp.exp(sc[0] - top[0][0]) / z
        e2_ref[rows, :] = jnp.exp(sc[1] - top[1][0])


def _peer_route(cfg, hn, wq, keys, tb):
    t, d = hn.shape
    hk = cfg.peer_heads * cfg.peer_n_keys
    const = lambda a: pl.BlockSpec(a.shape, lambda i: (0,) * a.ndim, pipeline_mode=pl.Buffered(1))
    col = lambda r: pl.BlockSpec((r, tb), lambda i: (0, i))
    return pl.pallas_call(
        functools.partial(_peer_route_kernel, cfg),
        grid=(t // tb,),
        in_specs=[pl.BlockSpec((tb, d), lambda i: (i, 0)), const(wq), const(keys)],
        out_specs=[col(hk)] * 4,
        out_shape=[jax.ShapeDtypeStruct((hk, t), F32)] * 4,
        compiler_params=_params(1),
        name="peer_route",
    )(hn, wq, keys)


def _gelu(x):
    return 0.5 * x * (1.0 + lax.erf(x * (2.0 ** -0.5)))


def _peer_dense_kernel(cfg, ib, hn_ref, h_ref, u_ref, vt_ref, c1_ref, r2_ref, e1_ref, e2_ref,
                       y_ref, acc_ref, act_ref, p_ref):
    nk = cfg.peer_n_keys
    ei = pl.program_id(1)
    tb = hn_ref.shape[0]

    @pl.when(ei == 0)
    def _():
        acc_ref[...] = jnp.zeros(acc_ref.shape, F32)

    act_ref[...] = _gelu(_dot_nt(u_ref[...], hn_ref[...]))

    n_row_tiles = nk // GATE_ROWS

    def gate_tile(n, carry):
        r0 = pl.multiple_of((n % n_row_tiles) * GATE_ROWS, GATE_ROWS)
        lanes = pl.ds(pl.multiple_of((n // n_row_tiles) * LANES, LANES), LANES)
        g = [jnp.zeros((GATE_ROWS, LANES), F32) for _ in range(ib)]
        for h in range(cfg.peer_heads):
            rows = pl.ds(h * nk + r0, GATE_ROWS)
            r2, e2 = r2_ref[rows, lanes], e2_ref[rows, lanes]
            for i_loc in range(ib):
                keep = r2 < c1_ref[h, i_loc:i_loc + 1, lanes]
                g[i_loc] = g[i_loc] + jnp.where(keep, e2, 0.0) * e1_ref[h, i_loc:i_loc + 1, lanes]
        for i_loc in range(ib):
            rows = pl.ds(i_loc * nk + r0, GATE_ROWS)
            p_ref[rows, lanes] = (g[i_loc] * act_ref[rows, lanes]).astype(BF16)
        return carry

    lax.fori_loop(0, n_row_tiles * (tb // LANES), gate_tile, 0)
    acc_ref[...] += _dot(vt_ref[...], p_ref[...])

    @pl.when(ei == pl.num_programs(1) - 1)
    def _():
        y_ref[...] = h_ref[...] + acc_ref[...].T


def _peer_dense(cfg, hn, h, u, vt, route, tb, ib):
    t, d = hn.shape
    nk = cfg.peer_n_keys
    eb = ib * nk
    hk = cfg.peer_heads * nk
    c1, r2, e1, e2 = route
    by_head = lambda a: a.reshape(cfg.peer_heads, nk, t)
    tokb = pl.BlockSpec((tb, d), lambda ti, ei: (ti, 0))
    col = lambda r: pl.BlockSpec((r, tb), lambda ti, ei: (0, ti))
    first = pl.BlockSpec((cfg.peer_heads, ib, tb), lambda ti, ei: (0, ei, ti))
    return pl.pallas_call(
        functools.partial(_peer_dense_kernel, cfg, ib),
        grid=(t // tb, cfg.n_experts // eb),
        in_specs=[tokb, tokb,
                  pl.BlockSpec((eb, d), lambda ti, ei: (ei, 0)),
                  pl.BlockSpec((d, eb), lambda ti, ei: (0, ei)),
                  first, col(hk), first, col(hk)],
        out_specs=tokb,
        out_shape=jax.ShapeDtypeStruct((t, d), F32),
        scratch_shapes=[pltpu.VMEM((d, tb), F32), pltpu.VMEM((eb, tb), F32), pltpu.VMEM((eb, tb), BF16)],
        compiler_params=_params(2, arbitrary_last=True),
        name="peer_dense",
    )(hn, h, u, vt, by_head(c1), r2, by_head(e1), e2)


def _prep_params(cfg, norm1_g, w_in, qk_norm_g, da_lambda, da_subln_g, gla_w_gate2, gla_b_gate, gla_norm_g,
                 w_branch_a, w_branch_b, w_out, norm2_g, peer_w_q, peer_sub_keys, peer_u, peer_v):
    d, qw, gqk, gvw, rank = cfg.d_model, cfg.da_w, cfg.gla_qk_w, cfg.gla_v_w, cfg.gla_gate_rank
    splits = (qw, qw, qw, gqk, gqk, gvw, rank, gvw, d, d)
    starts = np.concatenate([[0], np.cumsum(splits)])
    piece = lambda n: w_in[:, starts[n]:starts[n + 1]]
    glr = jnp.pad(piece(6), ((0, 0), (0, LANES - rank)))
    w_all = jnp.concatenate([piece(n) for n in (0, 1, 2, 3, 4, 5, 7, 8, 9)] + [glr], axis=1).astype(BF16)
    group = np.arange(qw) // cfg.da_head_dim
    seg = (group[:, None] == np.arange(LANES)[None, :]).astype(np.float32)
    return {
        "g1": norm1_g[None, :], "w_all": w_all,
        "qk_gain": jnp.tile(qk_norm_g, (1, qw // cfg.da_head_dim)),
        "seg": jnp.asarray(seg, BF16), "segt": jnp.asarray(seg.T, BF16),
        "w_gate2": jnp.pad(gla_w_gate2, ((0, LANES - rank), (0, 0))).astype(BF16),
        "b_gate": gla_b_gate[None, :],
        "da_lambda": da_lambda, "subln_g": da_subln_g[None, :], "gla_norm_g": gla_norm_g[None, :],
        "w_a": w_branch_a.astype(BF16), "w_b": w_branch_b.astype(BF16), "w_o": w_out.astype(BF16),
        "g2": norm2_g[None, :],
        "wq": peer_w_q.astype(BF16),
        "keys": peer_sub_keys.reshape(cfg.peer_heads * 2, cfg.peer_n_keys, cfg.peer_qdim // 2).astype(BF16),
        "u": peer_u.astype(BF16), "vt": peer_v.astype(BF16).T,
    }


def _pick(n, target):
    t = min(n, target)
    while n % t:
        t //= 2
    return t


def _layer(cfg, x, cache_k, cache_v, s0, p, lam_init):
    b, l, d = x.shape
    t = b * l
    past = 0 if cache_k is None else cache_k.shape[1]
    qw = cfg.da_w

    tm = _pick(t, 256)
    if l % tm == 0:
        positions = past + jnp.arange(l)
    else:
        positions = past + (jnp.arange(tm) % l)
    cos, sin = _rope_tables(cfg, positions)
    q, k, kb, v, vb, gq, gk, gv, la, gr, ga, gb = _in_proj(cfg, x.reshape(t, d), cos, sin, p, tm)

    r3 = lambda a: a.reshape(b, l, a.shape[-1])
    if cache_k is None:
        kk, vv = r3(kb), r3(vb)
        tq = _pick(l, 256)
        tk = tq
    else:
        tq = _pick(l, 256)
        tk = -(-(past + l) // LANES) * LANES
        pad = jnp.zeros((b, tk - past - l, qw), BF16)
        kk = jnp.concatenate([cache_k.reshape(b, past, qw).astype(BF16), r3(kb), pad], axis=1)
        vv = jnp.concatenate([cache_v.reshape(b, past, qw).astype(BF16), r3(vb), pad], axis=1)
    oa = _diff_attn(cfg, r3(q), kk, vv, p["da_lambda"], p["subln_g"], past, tq, tk, lam_init)

    if s0 is None:
        s0 = jnp.zeros((b, cfg.gla_heads, cfg.gla_dk, cfg.gla_dv), F32)
    ob, s_new = _gla(cfg, r3(gq), r3(gk), r3(gv), r3(la), s0, p["gla_norm_g"], _pick(l, 512))

    h, hn = _merge_out(cfg, x.reshape(t, d), oa.reshape(t, qw), ob.reshape(t, -1), gr, ga, gb, p, tm)

    tb = _pick(t, 512)
    route = _peer_route(cfg, hn, p["wq"], p["keys"], _pick(t, 256))
    y = _peer_dense(cfg, hn, h, p["u"], p["vt"], route, tb, 8)

    k_new = k.reshape(b, l, cfg.da_heads, 2, cfg.da_head_dim)
    v_new = v.reshape(b, l, cfg.da_heads, cfg.da_v_dim)
    return y.reshape(b, l, d), k_new, v_new, s_new


def _forward(cfg, x_prompt, x_sample, cache_k, cache_v, state_gla, *weights):
    yp, ys = x_prompt, x_sample
    outs = [[] for _ in range(6)]
    for layer in range(cache_k.shape[0]):
        p = _prep_params(cfg, *[w[layer] for w in weights])
        lam_init = 0.8 - 0.6 * math.exp(-0.3 * layer)
        yp, kp, vp, sp = _layer(cfg, yp, None, None, None, p, lam_init)
        ys, kn, vn, sn = _layer(cfg, ys, cache_k[layer], cache_v[layer], state_gla[layer], p, lam_init)
        for lst, a in zip(outs, (kp, vp, sp, kn, vn, sn)):
            lst.append(a)
    return (yp, ys) + tuple(jnp.stack(o) for o in outs)


def kernel(x_prompt, x_sample, cache_k, cache_v, state_gla, norm1_g, w_in, qk_norm_g, da_lambda, da_subln_g,
           gla_w_gate2, gla_b_gate, gla_norm_g, w_branch_a, w_branch_b, w_out, norm2_g, peer_w_q,
           peer_sub_keys, peer_u, peer_v):
    return _forward(Cfg(), x_prompt, x_sample, cache_k, cache_v, state_gla, norm1_g, w_in, qk_norm_g, da_lambda,
                    da_subln_g, gla_w_gate2, gla_b_gate, gla_norm_g, w_branch_a, w_branch_b, w_out, norm2_g,
                    peer_w_q, peer_sub_keys, peer_u, peer_v)
```

```python
import functools
import math
from typing import NamedTuple

import jax
import jax.numpy as jnp
import numpy as np
from jax import lax
from jax.experimental import pallas as pl
from jax.experimental.pallas import tpu as pltpu

F32 = jnp.float32
BF16 = jnp.bfloat16
LANES = 128
VMEM_LIMIT = 56 * 1024 * 1024
GATE_ROWS = 16
ATTN_TQ = 512
ATTN_TK = 512


class Cfg(NamedTuple):
    d_model: int = 1024
    chunk: int = 64
    eps: float = 1e-6
    da_heads: int = 8
    da_head_dim: int = 64
    rope_theta: float = 10000.0
    gla_heads: int = 4
    gla_gate_rank: int = 16
    gla_gate_tau: float = 16.0
    peer_heads: int = 8
    peer_n_keys: int = 128
    peer_qdim: int = 256
    peer_topk: int = 16

    @property
    def da_v_dim(self): return 2 * self.da_head_dim
    @property
    def da_w(self): return self.da_heads * 2 * self.da_head_dim
    @property
    def gla_dk(self): return self.d_model // 2 // self.gla_heads
    @property
    def gla_dv(self): return self.d_model // self.gla_heads
    @property
    def gla_qk_w(self): return self.gla_heads * self.gla_dk
    @property
    def gla_v_w(self): return self.gla_heads * self.gla_dv
    @property
    def n_experts(self): return self.peer_n_keys * self.peer_n_keys


def _dot(a, b):
    return jnp.dot(a, b, preferred_element_type=F32)


def _dot_nt(a, b):
    return lax.dot_general(a, b, (((1,), (1,)), ((), ())), preferred_element_type=F32)


def _split_bf16(t):
    hi = t.astype(BF16)
    lo = (t - hi.astype(F32)).astype(BF16)
    return hi, lo


def _params(n_axes, arbitrary_last=False):
    sem = ["parallel"] * n_axes
    if arbitrary_last:
        sem[-1] = "arbitrary"
    return pltpu.CompilerParams(dimension_semantics=tuple(sem), vmem_limit_bytes=VMEM_LIMIT)


def _rope_table_kernel(pos_ref, freq_ref, sign_ref, cos_ref, sin_ref):
    ang = pos_ref[...] * freq_ref[...]
    cos_ref[...] = jnp.cos(ang)
    sin_ref[...] = jnp.sin(ang) * sign_ref[...]


def _rope_tables(cfg, positions):
    half = cfg.da_head_dim // 2
    r = positions.shape[0]
    lane = np.arange(LANES)
    inv_freq = cfg.rope_theta ** (-jnp.arange(half, dtype=F32) / half)
    freq = jnp.tile(inv_freq, LANES // half)[None, :]
    sign = jnp.asarray(np.where(lane % cfg.da_head_dim < half, -1.0, 1.0), F32)[None, :]
    pos = jnp.broadcast_to(positions.astype(F32)[:, None], (r, LANES))
    full = lambda shape: pl.BlockSpec(shape, lambda: (0, 0))
    return pl.pallas_call(
        _rope_table_kernel,
        out_shape=(jax.ShapeDtypeStruct((r, LANES), F32),) * 2,
        in_specs=[full((r, LANES)), full((1, LANES)), full((1, LANES))],
        out_specs=(full((r, LANES)),) * 2,
        name="rope_table",
    )(pos, freq, sign)


def _log_sigmoid(x):
    return jnp.minimum(x, 0.0) - jnp.log1p(jnp.exp(-jnp.abs(x)))


def _in_proj_kernel(cfg, x_ref, g1_ref, w_ref, cos_ref, sin_ref, qkg_ref, seg_ref, segt_ref, w2_ref, b2_ref,
                    q_out, k_out, kb_out, v_out, vb_out, gq_out, gk_out, gv_out, la_out, gr_out, ga_out, gb_out):
    d = cfg.d_model
    qw, gqk, gvw = cfg.da_w, cfg.gla_qk_w, cfg.gla_v_w
    x = x_ref[...]
    xn = (x * lax.rsqrt(jnp.mean(x * x, axis=-1, keepdims=True) + cfg.eps) * g1_ref[...]).astype(BF16)

    off = [0]

    def proj(width):
        y = _dot(xn, w_ref[:, off[0]:off[0] + width])
        off[0] += width
        return y

    nrep = qw // LANES
    cos = jnp.tile(cos_ref[...], (1, nrep))
    sin = jnp.tile(sin_ref[...], (1, nrep))
    lane = lax.broadcasted_iota(jnp.int32, (x.shape[0], qw), 1)
    first_half = (lane & (cfg.da_head_dim - 1)) < (cfg.da_head_dim // 2)
    seg = seg_ref[...]
    segt = segt_ref[...]

    def qk_norm_rope(t, gain):
        hi, lo = _split_bf16(t * t)
        ss = _dot(hi, seg) + _dot(lo, seg)
        rhi, rlo = _split_bf16(lax.rsqrt(ss * (1.0 / cfg.da_head_dim) + cfg.eps))
        y = t * (_dot(rhi, segt) + _dot(rlo, segt)) * gain
        half = cfg.da_head_dim // 2
        partner = jnp.where(first_half, pltpu.roll(y, qw - half, 1), pltpu.roll(y, half, 1))
        return y * cos + partner * sin

    q = qk_norm_rope(proj(qw), qkg_ref[0:1, :])
    q_out[...] = (q * (cfg.da_head_dim ** -0.5)).astype(BF16)
    k = qk_norm_rope(proj(qw), qkg_ref[1:2, :])
    k_out[...] = k
    kb_out[...] = k.astype(BF16)
    v = proj(qw)
    v_out[...] = v
    vb_out[...] = v.astype(BF16)
    gq_out[...] = proj(gqk) * (cfg.gla_dk ** -0.5)
    gk_out[...] = proj(gqk)
    gv_out[...] = proj(gvw)
    gr = proj(gvw)
    gr_out[...] = gr * jax.nn.sigmoid(gr)
    ga_out[...] = jax.nn.sigmoid(proj(d))
    gb_out[...] = jax.nn.sigmoid(proj(d))
    glr = proj(LANES)
    gate = _dot(glr.astype(BF16), w2_ref[...]) + b2_ref[...]
    la_out[...] = _log_sigmoid(gate) * (1.0 / cfg.gla_gate_tau)


def _in_proj(cfg, x, cos, sin, p, tm):
    t, d = x.shape
    qw, gqk, gvw = cfg.da_w, cfg.gla_qk_w, cfg.gla_v_w
    n_tab = cos.shape[0] // tm
    tok = lambda w: pl.BlockSpec((tm, w), lambda i: (i, 0))
    const = lambda a: pl.BlockSpec(a.shape, lambda i: (0,) * a.ndim, pipeline_mode=pl.Buffered(1))
    tab = pl.BlockSpec((tm, LANES), lambda i: (i % n_tab, 0))
    consts = [p["g1"], p["w_all"]]
    consts2 = [p["qk_gain"], p["seg"], p["segt"], p["w_gate2"], p["b_gate"]]
    outs = [(qw, BF16), (qw, F32), (qw, BF16), (qw, F32), (qw, BF16), (gqk, F32), (gqk, F32), (gvw, F32),
            (gqk, F32), (gvw, F32), (d, F32), (d, F32)]
    return pl.pallas_call(
        functools.partial(_in_proj_kernel, cfg),
        grid=(t // tm,),
        in_specs=[tok(d)] + [const(a) for a in consts] + [tab, tab] + [const(a) for a in consts2],
        out_specs=[tok(w) for w, _ in outs],
        out_shape=[jax.ShapeDtypeStruct((t, w), dt) for w, dt in outs],
        compiler_params=_params(1),
        name="in_proj",
    )(x, *consts, cos, sin, *consts2)


def _diff_attn_kernel(cfg, past, tq, tk, lam_init, q_ref, k_ref, v_ref, lamp_ref, g_ref, o_ref,
                      s_ref, mpart_ref, mrow_ref, lpart_ref, acc_ref):
    qi = pl.program_id(2)
    hd, vd = cfg.da_head_dim, cfg.da_v_dim
    shift = int(math.log2(cfg.chunk))
    n_lane_chunks = tk // LANES
    q = q_ref[0]
    lane = lax.broadcasted_iota(jnp.int32, q.shape, 1)
    zero = jnp.zeros_like(q)
    qs = jnp.concatenate([jnp.where(lane < hd, q, zero), jnp.where(lane >= hd, q, zero)], axis=0)

    q_start = past + qi * tq
    n_blocks = (q_start + tq + tk - 1) // tk
    n_full = (q_start + cfg.chunk) // tk

    mpart_ref[...] = jnp.full(mpart_ref.shape, -jnp.inf, F32)

    def scores(j, masked):
        ks = pl.multiple_of(j * tk, tk)
        s = _dot_nt(qs, k_ref[0, pl.ds(ks, tk), :])
        if masked:
            row = lax.broadcasted_iota(jnp.int32, s.shape, 0)
            col = lax.broadcasted_iota(jnp.int32, s.shape, 1)
            qpos = q_start + jnp.where(row >= tq, row - tq, row)
            allowed = lax.shift_right_logical(ks + col, shift) <= lax.shift_right_logical(qpos, shift)
            s = jnp.where(allowed, s, -jnp.inf)
        s_ref[j] = s
        part = s[:, 0:LANES]
        for c in range(1, n_lane_chunks):
            part = jnp.maximum(part, s[:, c * LANES:(c + 1) * LANES])
        mpart_ref[...] = jnp.maximum(mpart_ref[...], part)

    lax.fori_loop(0, n_full, lambda j, c: (scores(j, False), c)[1], 0)
    lax.fori_loop(n_full, n_blocks, lambda j, c: (scores(j, True), c)[1], 0)

    mrow_ref[...] = jnp.broadcast_to(jnp.max(mpart_ref[...], axis=-1, keepdims=True), mrow_ref.shape)
    lpart_ref[...] = jnp.zeros(lpart_ref.shape, F32)
    acc_ref[...] = jnp.zeros(acc_ref.shape, F32)

    def weigh(j, carry):
        ks = pl.multiple_of(j * tk, tk)
        m = mrow_ref[...]
        ps = [jnp.exp(s_ref[j, :, c * LANES:(c + 1) * LANES] - m) for c in range(n_lane_chunks)]
        lsum = ps[0]
        for pc in ps[1:]:
            lsum = lsum + pc
        lpart_ref[...] += lsum
        p = jnp.concatenate([pc.astype(BF16) for pc in ps], axis=1)
        acc_ref[...] += _dot(p, v_ref[0, pl.ds(ks, tk), :])
        return carry

    lax.fori_loop(0, n_blocks, weigh, 0)

    lp = lamp_ref[...]
    lam = (jnp.exp(jnp.sum(lp[0:1] * lp[1:2], axis=-1, keepdims=True))
           - jnp.exp(jnp.sum(lp[2:3] * lp[3:4], axis=-1, keepdims=True)) + lam_init)
    o = acc_ref[...] / jnp.sum(lpart_ref[...], axis=-1, keepdims=True)
    o = o[:tq] - lam * o[tq:]
    o = o * lax.rsqrt(jnp.mean(o * o, axis=-1, keepdims=True) + cfg.eps) * g_ref[...] * (1.0 - lam_init)
    o_ref[0] = o.astype(BF16)


def _diff_attn(cfg, q, k, v, lam_p, subln_g, past, tq, tk, lam_init):
    b, lq, w = q.shape
    lk = k.shape[1]
    vd = cfg.da_v_dim
    assert lq % tq == 0 and lk % tk == 0 and tk % LANES == 0 and tq % cfg.chunk == 0 and past % cfg.chunk == 0
    assert past + lq <= lk
    kv_spec = pl.BlockSpec((1, lk, vd), lambda bi, h, qi: (bi, 0, h))
    q_spec = pl.BlockSpec((1, tq, vd), lambda bi, h, qi: (bi, qi, h))
    return pl.pallas_call(
        functools.partial(_diff_attn_kernel, cfg, past, tq, tk, lam_init),
        grid=(b, cfg.da_heads, lq // tq),
        in_specs=[q_spec, kv_spec, kv_spec,
                  pl.BlockSpec(lam_p.shape, lambda bi, h, qi: (0, 0)),
                  pl.BlockSpec(subln_g.shape, lambda bi, h, qi: (0, 0))],
        out_specs=q_spec,
        out_shape=jax.ShapeDtypeStruct((b, lq, w), BF16),
        scratch_shapes=[pltpu.VMEM((lk // tk, 2 * tq, tk), F32), pltpu.VMEM((2 * tq, LANES), F32),
                        pltpu.VMEM((2 * tq, LANES), F32), pltpu.VMEM((2 * tq, LANES), F32),
                        pltpu.VMEM((2 * tq, vd), F32)],
        compiler_params=_params(3),
        name="diff_attn",
    )(q, k, v, lam_p, subln_g)


def _gla_constants(c):
    r = np.arange(c)
    i = r[:, None]
    rr = r[None, :]
    mats = [(rr <= i), (rr > i)]
    levels = []
    s = c // 2
    while s >= 1:
        levels.append(s)
        s //= 2
    for s in levels:
        bd = (i // s) * s
        mats.append((rr > bd) & (rr <= i))
        bd2 = (i // s + 1) * s
        mats.append((rr > i) & (rr <= bd2))
    stack = np.concatenate([m.astype(np.float32) for m in mats], axis=0)
    lev = np.full((c, c), -1, np.int32)
    jj = r[None, :]
    for n, s in enumerate(levels):
        sel = (jj < i) & (i // s != jj // s) & (lev < 0)
        lev[sel] = n
    lev[i == jj] = len(levels)
    return stack, lev, len(levels)


def _gla_kernel(cfg, n_levels, n_sub, q_ref, k_ref, v_ref, a_ref, s0_ref, mat_ref, lev_ref, g_ref,
                o_ref, s_out_ref, st_ref):
    c, dk, dv = cfg.chunk, cfg.gla_dk, cfg.gla_dv
    step_i = pl.program_id(1)

    @pl.when(step_i == 0)
    def _():
        for hd in range(cfg.gla_heads):
            st_ref[hd] = s0_ref[0, hd].T

    mat = mat_ref[...]
    lev = lev_ref[...]

    def head_chunk(hd, rows, e_all):
        kcols = slice(hd * dk, (hd + 1) * dk)
        vcols = slice(hd * dv, (hd + 1) * dv)
        q, k, v = q_ref[0, rows, kcols], k_ref[0, rows, kcols], v_ref[0, rows, vcols]
        e = e_all[:, kcols]
        b = e[0:c]
        to_last = e[c:2 * c]
        att = jnp.where(lev == n_levels, _dot_nt(q.astype(BF16), k.astype(BF16)), 0.0)
        for n_l in range(n_levels):
            eq = e[(2 + 2 * n_l) * c:(3 + 2 * n_l) * c]
            ek = e[(3 + 2 * n_l) * c:(4 + 2 * n_l) * c]
            a_l = _dot_nt((q * jnp.exp(eq)).astype(BF16), (k * jnp.exp(ek)).astype(BF16))
            att = jnp.where(lev == n_l, a_l, att)
        st = st_ref[hd]
        vb = v.astype(BF16)
        o = _dot_nt((q * jnp.exp(b)).astype(BF16), st.astype(BF16)) + _dot(att.astype(BF16), vb)
        kl = (k * jnp.exp(to_last)).astype(BF16)
        upd = lax.dot_general(vb, kl, (((0,), (0,)), ((), ())), preferred_element_type=F32)
        st_ref[hd] = st * jnp.exp(b[c - 1:c, :]) + upd
        o = o * lax.rsqrt(jnp.mean(o * o, axis=-1, keepdims=True) + cfg.eps) * g_ref[...]
        o_ref[0, rows, vcols] = o

    def chunk(n, carry):
        rows = pl.ds(pl.multiple_of(n * c, c), c)
        a_hi, a_lo = _split_bf16(a_ref[0, rows, :])
        e_all = _dot(mat, a_hi) + _dot(mat, a_lo)
        for hd in range(cfg.gla_heads):
            head_chunk(hd, rows, e_all)
        return carry

    lax.fori_loop(0, n_sub, chunk, 0)

    @pl.when(step_i == pl.num_programs(1) - 1)
    def _():
        for hd in range(cfg.gla_heads):
            s_out_ref[0, hd] = st_ref[hd].T


def _gla(cfg, gq, gk, gv, la, s0, norm_g, lb):
    b, l, _ = gq.shape
    c, dk, dv, h = cfg.chunk, cfg.gla_dk, cfg.gla_dv, cfg.gla_heads
    assert l % lb == 0 and lb % c == 0
    stack, lev, n_levels = _gla_constants(c)
    mat = jnp.asarray(stack, BF16)
    lev = jnp.asarray(lev)
    seq = lambda w: pl.BlockSpec((1, lb, h * w), lambda bi, si: (bi, si, 0))
    state = pl.BlockSpec((1, h, dk, dv), lambda bi, si: (bi, 0, 0, 0))
    const = lambda a: pl.BlockSpec(a.shape, lambda bi, si: (0,) * a.ndim)
    return pl.pallas_call(
        functools.partial(_gla_kernel, cfg, n_levels, lb // c),
        grid=(b, l // lb),
        in_specs=[seq(dk), seq(dk), seq(dv), seq(dk), state, const(mat), const(lev), const(norm_g)],
        out_specs=[seq(dv), state],
        out_shape=[jax.ShapeDtypeStruct((b, l, h * dv), F32), jax.ShapeDtypeStruct((b, h, dk, dv), F32)],
        scratch_shapes=[pltpu.VMEM((h, dv, dk), F32)],
        compiler_params=_params(2, arbitrary_last=True),
        name="gla",
    )(gq, gk, gv, la, s0, mat, lev, norm_g)


def _merge_kernel(cfg, x_ref, oa_ref, ob_ref, gr_ref, ga_ref, gb_ref, wa_ref, wb_ref, wo_ref, g2_ref,
                  h_ref, hn_ref):
    a = _dot(oa_ref[...], wa_ref[...])
    bb = _dot((ob_ref[...] * gr_ref[...]).astype(BF16), wb_ref[...])
    mix = ga_ref[...] * a + gb_ref[...] * bb
    h = x_ref[...] + _dot(mix.astype(BF16), wo_ref[...])
    h_ref[...] = h
    hn = h * lax.rsqrt(jnp.mean(h * h, axis=-1, keepdims=True) + cfg.eps) * g2_ref[...]
    hn_ref[...] = hn.astype(BF16)


def _merge_out(cfg, x, oa, ob, gr, ga, gb, p, tm):
    t, d = x.shape
    tok = lambda a: pl.BlockSpec((tm, a.shape[1]), lambda i: (i, 0))
    const = lambda a: pl.BlockSpec(a.shape, lambda i: (0,) * a.ndim, pipeline_mode=pl.Buffered(1))
    toks = [x, oa, ob, gr, ga, gb]
    consts = [p["w_a"], p["w_b"], p["w_o"], p["g2"]]
    return pl.pallas_call(
        functools.partial(_merge_kernel, cfg),
        grid=(t // tm,),
        in_specs=[tok(a) for a in toks] + [const(a) for a in consts],
        out_specs=[pl.BlockSpec((tm, d), lambda i: (i, 0))] * 2,
        out_shape=[jax.ShapeDtypeStruct((t, d), F32), jax.ShapeDtypeStruct((t, d), BF16)],
        compiler_params=_params(1),
        name="merge_out",
    )(*toks, *consts)


def _top_values(x, k):
    vals = []
    rank = jnp.full(x.shape, float(k), F32)
    for n in range(k):
        m = jnp.max(x, axis=0, keepdims=True)
        vals.append(m)
        hit = x == m
        rank = jnp.where(hit, float(n), rank)
        x = jnp.where(hit, -jnp.inf, x)
    return vals, rank


def _rows_to_block(rows):
    k, t = len(rows), rows[0].shape[1]
    idx = lax.broadcasted_iota(jnp.int32, (k, t), 0)
    out = jnp.zeros((k, t), F32)
    for n, r in enumerate(rows):
        out = jnp.where(idx == n, r, out)
    return out


def _candidate_sums(top0, top1, k):
    tile = 8
    assert k % tile == 0 and k // (tile + 1) == 1
    blk0, blk1 = _rows_to_block(top0), _rows_to_block(top1)
    row = lax.broadcasted_iota(jnp.int32, (tile, top0[0].shape[1]), 0)
    pieces = []
    for b in range(tile):
        n_a = k // (b + 1)
        for a0 in range(0, n_a, tile):
            piece = blk0[a0:a0 + tile] + top1[b]
            pieces.append(piece if a0 + tile <= n_a else jnp.where(row < n_a - a0, piece, -jnp.inf))
    for b0 in range(tile, k, tile):
        pieces.append(blk1[b0:b0 + tile] + top0[0])
    return jnp.concatenate(pieces, axis=0)


def _peer_route_kernel(cfg, hn_ref, wq_ref, keys_ref, c1_ref, r2_ref, e1_ref, e2_ref):
    nk, topk = cfg.peer_n_keys, cfg.peer_topk
    sub = cfg.peer_qdim // 2
    q = _dot(hn_ref[...], wq_ref[...]).astype(BF16)
    for h in range(cfg.peer_heads):
        sc, top, rank = [], [], []
        for half in range(2):
            col = (2 * h + half) * sub
            s = _dot_nt(keys_ref[2 * h + half], q[:, col:col + sub])
            vals, rk = _top_values(s, topk)
            sc.append(s)
            top.append(vals)
            rank.append(rk)
        best, _ = _top_values(_candidate_sums(top[0], top[1], topk), topk)
        tau = best[topk - 1]
        z = best[0] * 0.0
        for cv in best:
            z = z + jnp.exp(cv - best[0])
        count = jnp.zeros(sc[0].shape, F32)
        for b in range(topk):
            count = count + jnp.where(sc[0] + top[1][b] >= tau, 1.0, 0.0)
        rows = pl.ds(h * nk, nk)
        c1_ref[rows, :] = count
        r2_ref[rows, :] = rank[1]
        e1_ref[rows, :] = jnp.exp(sc[0] - top[0][0]) / z
        e2_ref[rows, :] = jnp.exp(sc[1] - top[1][0])


def _peer_route(cfg, hn, wq, keys, tb):
    t, d = hn.shape
    hk = cfg.peer_heads * cfg.peer_n_keys
    const = lambda a: pl.BlockSpec(a.shape, lambda i: (0,) * a.ndim, pipeline_mode=pl.Buffered(1))
    col = lambda r: pl.BlockSpec((r, tb), lambda i: (0, i))
    return pl.pallas_call(
        functools.partial(_peer_route_kernel, cfg),
        grid=(t // tb,),
        in_specs=[pl.BlockSpec((tb, d), lambda i: (i, 0)), const(wq), const(keys)],
        out_specs=[col(hk)] * 4,
        out_shape=[jax.ShapeDtypeStruct((hk, t), F32)] * 4,
        compiler_params=_params(1),
        name="peer_route",
    )(hn, wq, keys)


def _gelu(x):
    return 0.5 * x * (1.0 + lax.erf(x * (2.0 ** -0.5)))


def _peer_dense_kernel(cfg, ib, hn_ref, h_ref, u_ref, vt_ref, c1_ref, r2_ref, e1_ref, e2_ref,
                       y_ref, acc_ref, act_ref, p_ref):
    nk = cfg.peer_n_keys
    ei = pl.program_id(1)
    tb = hn_ref.shape[0]

    @pl.when(ei == 0)
    def _():
        acc_ref[...] = jnp.zeros(acc_ref.shape, F32)

    act_ref[...] = _gelu(_dot_nt(u_ref[...], hn_ref[...]))

    n_row_tiles = nk // GATE_ROWS

    def gate_tile(n, carry):
        r0 = pl.multiple_of((n % n_row_tiles) * GATE_ROWS, GATE_ROWS)
        lanes = pl.ds(pl.multiple_of((n // n_row_tiles) * LANES, LANES), LANES)
        g = [jnp.zeros((GATE_ROWS, LANES), F32) for _ in range(ib)]
        for h in range(cfg.peer_heads):
            rows = pl.ds(h * nk + r0, GATE_ROWS)
            r2, e2 = r2_ref[rows, lanes], e2_ref[rows, lanes]
            for i_loc in range(ib):
                keep = r2 < c1_ref[h, i_loc:i_loc + 1, lanes]
                g[i_loc] = g[i_loc] + jnp.where(keep, e2, 0.0) * e1_ref[h, i_loc:i_loc + 1, lanes]
        for i_loc in range(ib):
            rows = pl.ds(i_loc * nk + r0, GATE_ROWS)
            p_ref[rows, lanes] = (g[i_loc] * act_ref[rows, lanes]).astype(BF16)
        return carry

    lax.fori_loop(0, n_row_tiles * (tb // LANES), gate_tile, 0)
    acc_ref[...] += _dot(vt_ref[...], p_ref[...])

    @pl.when(ei == pl.num_programs(1) - 1)
    def _():
        y_ref[...] = h_ref[...] + acc_ref[...].T


def _peer_dense(cfg, hn, h, u, vt, route, tb, ib):
    t, d = hn.shape
    nk = cfg.peer_n_keys
    eb = ib * nk
    hk = cfg.peer_heads * nk
    c1, r2, e1, e2 = route
    by_head = lambda a: a.reshape(cfg.peer_heads, nk, t)
    tokb = pl.BlockSpec((tb, d), lambda ti, ei: (ti, 0))
    col = lambda r: pl.BlockSpec((r, tb), lambda ti, ei: (0, ti))
    first = pl.BlockSpec((cfg.peer_heads, ib, tb), lambda ti, ei: (0, ei, ti))
    return pl.pallas_call(
        functools.partial(_peer_dense_kernel, cfg, ib),
        grid=(t // tb, cfg.n_experts // eb),
        in_specs=[tokb, tokb,
                  pl.BlockSpec((eb, d), lambda ti, ei: (ei, 0)),
                  pl.BlockSpec((d, eb), lambda ti, ei: (0, ei)),
                  first, col(hk), first, col(hk)],
        out_specs=tokb,
        out_shape=jax.ShapeDtypeStruct((t, d), F32),
        scratch_shapes=[pltpu.VMEM((d, tb), F32), pltpu.VMEM((eb, tb), F32), pltpu.VMEM((eb, tb), BF16)],
        compiler_params=_params(2, arbitrary_last=True),
        name="peer_dense",
    )(hn, h, u, vt, by_head(c1), r2, by_head(e1), e2)


def _prep_params(cfg, norm1_g, w_in, qk_norm_g, da_lambda, da_subln_g, gla_w_gate2, gla_b_gate, gla_norm_g,
                 w_branch_a, w_branch_b, w_out, norm2_g, peer_w_q, peer_sub_keys, peer_u, peer_v):
    d, qw, gqk, gvw, rank = cfg.d_model, cfg.da_w, cfg.gla_qk_w, cfg.gla_v_w, cfg.gla_gate_rank
    splits = (qw, qw, qw, gqk, gqk, gvw, rank, gvw, d, d)
    starts = np.concatenate([[0], np.cumsum(splits)])
    piece = lambda n: w_in[:, starts[n]:starts[n + 1]]
    glr = jnp.pad(piece(6), ((0, 0), (0, LANES - rank)))
    w_all = jnp.concatenate([piece(n) for n in (0, 1, 2, 3, 4, 5, 7, 8, 9)] + [glr], axis=1).astype(BF16)
    group = np.arange(qw) // cfg.da_head_dim
    seg = (group[:, None] == np.arange(LANES)[None, :]).astype(np.float32)
    return {
        "g1": norm1_g[None, :], "w_all": w_all,
        "qk_gain": jnp.tile(qk_norm_g, (1, qw // cfg.da_head_dim)),
        "seg": jnp.asarray(seg, BF16), "segt": jnp.asarray(seg.T, BF16),
        "w_gate2": jnp.pad(gla_w_gate2, ((0, LANES - rank), (0, 0))).astype(BF16),
        "b_gate": gla_b_gate[None, :],
        "da_lambda": da_lambda, "subln_g": da_subln_g[None, :], "gla_norm_g": gla_norm_g[None, :],
        "w_a": w_branch_a.astype(BF16), "w_b": w_branch_b.astype(BF16), "w_o": w_out.astype(BF16),
        "g2": norm2_g[None, :],
        "wq": peer_w_q.astype(BF16),
        "keys": peer_sub_keys.reshape(cfg.peer_heads * 2, cfg.peer_n_keys, cfg.peer_qdim // 2).astype(BF16),
        "u": peer_u.astype(BF16), "vt": peer_v.astype(BF16).T,
    }


def _pick(n, target):
    t = min(n, target)
    while n % t:
        t //= 2
    return t


def _layer(cfg, x, cache_k, cache_v, s0, p, lam_init):
    b, l, d = x.shape
    t = b * l
    past = 0 if cache_k is None else cache_k.shape[1]
    qw = cfg.da_w

    tm = _pick(t, 256)
    if l % tm == 0:
        positions = past + jnp.arange(l)
    else:
        positions = past + (jnp.arange(tm) % l)
    cos, sin = _rope_tables(cfg, positions)
    q, k, kb, v, vb, gq, gk, gv, la, gr, ga, gb = _in_proj(cfg, x.reshape(t, d), cos, sin, p, tm)

    r3 = lambda a: a.reshape(b, l, a.shape[-1])
    if cache_k is None:
        kk, vv = r3(kb), r3(vb)
        tq = _pick(l, ATTN_TQ)
        tk = _pick(l, ATTN_TK)
    else:
        tq = _pick(l, ATTN_TQ)
        tk = -(-(past + l) // LANES) * LANES
        pad = jnp.zeros((b, tk - past - l, qw), BF16)
        kk = jnp.concatenate([cache_k.reshape(b, past, qw).astype(BF16), r3(kb), pad], axis=1)
        vv = jnp.concatenate([cache_v.reshape(b, past, qw).astype(BF16), r3(vb), pad], axis=1)
    oa = _diff_attn(cfg, r3(q), kk, vv, p["da_lambda"], p["subln_g"], past, tq, tk, lam_init)

    if s0 is None:
        s0 = jnp.zeros((b, cfg.gla_heads, cfg.gla_dk, cfg.gla_dv), F32)
    ob, s_new = _gla(cfg, r3(gq), r3(gk), r3(gv), r3(la), s0, p["gla_norm_g"], _pick(l, 512))

    h, hn = _merge_out(cfg, x.reshape(t, d), oa.reshape(t, qw), ob.reshape(t, -1), gr, ga, gb, p, tm)

    tb = _pick(t, 512)
    route = _peer_route(cfg, hn, p["wq"], p["keys"], _pick(t, 256))
    y = _peer_dense(cfg, hn, h, p["u"], p["vt"], route, tb, 8)

    k_new = k.reshape(b, l, cfg.da_heads, 2, cfg.da_head_dim)
    v_new = v.reshape(b, l, cfg.da_heads, cfg.da_v_dim)
    return y.reshape(b, l, d), k_new, v_new, s_new


def _forward(cfg, x_prompt, x_sample, cache_k, cache_v, state_gla, *weights):
    yp, ys = x_prompt, x_sample
    outs = [[] for _ in range(6)]
    for layer in range(cache_k.shape[0]):
        p = _prep_params(cfg, *[w[layer] for w in weights])
        lam_init = 0.8 - 0.6 * math.exp(-0.3 * layer)
        yp, kp, vp, sp = _layer(cfg, yp, None, None, None, p, lam_init)
        ys, kn, vn, sn = _layer(cfg, ys, cache_k[layer], cache_v[layer], state_gla[layer], p, lam_init)
        for lst, a in zip(outs, (kp, vp, sp, kn, vn, sn)):
            lst.append(a)
    return (yp, ys) + tuple(jnp.stack(o) for o in outs)


def kernel(x_prompt, x_sample, cache_k, cache_v, state_gla, norm1_g, w_in, qk_norm_g, da_lambda, da_subln_g,
           gla_w_gate2, gla_b_gate, gla_norm_g, w_branch_a, w_branch_b, w_out, norm2_g, peer_w_q,
           peer_sub_keys, peer_u, peer_v):
    return _forward(Cfg(), x_prompt, x_sample, cache_k, cache_v, state_gla, norm1_g, w_in, qk_norm_g, da_lambda,
                    da_subln_g, gla_w_gate2, gla_b_gate, gla_norm_g, w_branch_a, w_branch_b, w_out, norm2_g,
                    peer_w_q, peer_sub_keys, peer_u, peer_v)
```

```python
import functools
import math
from typing import NamedTuple

import jax
import jax.numpy as jnp
import numpy as np
from jax import lax
from jax.experimental import pallas as pl
from jax.experimental.pallas import tpu as pltpu

F32 = jnp.float32
BF16 = jnp.bfloat16
LANES = 128
VMEM_LIMIT = 56 * 1024 * 1024
GATE_ROWS = 16
ROUTE_TOKENS = 256
PEER_FIRST_PER_STEP = 16
GATE_FIRST = 8
ATTN_TQ = 512
ATTN_TK = 512


class Cfg(NamedTuple):
    d_model: int = 1024
    chunk: int = 64
    eps: float = 1e-6
    da_heads: int = 8
    da_head_dim: int = 64
    rope_theta: float = 10000.0
    gla_heads: int = 4
    gla_gate_rank: int = 16
    gla_gate_tau: float = 16.0
    peer_heads: int = 8
    peer_n_keys: int = 128
    peer_qdim: int = 256
    peer_topk: int = 16

    @property
    def da_v_dim(self): return 2 * self.da_head_dim
    @property
    def da_w(self): return self.da_heads * 2 * self.da_head_dim
    @property
    def gla_dk(self): return self.d_model // 2 // self.gla_heads
    @property
    def gla_dv(self): return self.d_model // self.gla_heads
    @property
    def gla_qk_w(self): return self.gla_heads * self.gla_dk
    @property
    def gla_v_w(self): return self.gla_heads * self.gla_dv
    @property
    def n_experts(self): return self.peer_n_keys * self.peer_n_keys


def _dot(a, b):
    return jnp.dot(a, b, preferred_element_type=F32)


def _dot_nt(a, b):
    return lax.dot_general(a, b, (((1,), (1,)), ((), ())), preferred_element_type=F32)


def _split_bf16(t):
    hi = t.astype(BF16)
    lo = (t - hi.astype(F32)).astype(BF16)
    return hi, lo


def _params(n_axes, arbitrary_last=False):
    sem = ["parallel"] * n_axes
    if arbitrary_last:
        sem[-1] = "arbitrary"
    return pltpu.CompilerParams(dimension_semantics=tuple(sem), vmem_limit_bytes=VMEM_LIMIT)


def _rope_table_kernel(pos_ref, freq_ref, sign_ref, cos_ref, sin_ref):
    ang = pos_ref[...] * freq_ref[...]
    cos_ref[...] = jnp.cos(ang)
    sin_ref[...] = jnp.sin(ang) * sign_ref[...]


def _rope_tables(cfg, positions):
    half = cfg.da_head_dim // 2
    r = positions.shape[0]
    lane = np.arange(LANES)
    inv_freq = cfg.rope_theta ** (-jnp.arange(half, dtype=F32) / half)
    freq = jnp.tile(inv_freq, LANES // half)[None, :]
    sign = jnp.asarray(np.where(lane % cfg.da_head_dim < half, -1.0, 1.0), F32)[None, :]
    pos = jnp.broadcast_to(positions.astype(F32)[:, None], (r, LANES))
    full = lambda shape: pl.BlockSpec(shape, lambda: (0, 0))
    return pl.pallas_call(
        _rope_table_kernel,
        out_shape=(jax.ShapeDtypeStruct((r, LANES), F32),) * 2,
        in_specs=[full((r, LANES)), full((1, LANES)), full((1, LANES))],
        out_specs=(full((r, LANES)),) * 2,
        name="rope_table",
    )(pos, freq, sign)


def _log_sigmoid(x):
    return jnp.minimum(x, 0.0) - jnp.log1p(jnp.exp(-jnp.abs(x)))


def _in_proj_kernel(cfg, x_ref, g1_ref, w_ref, cos_ref, sin_ref, qkg_ref, seg_ref, segt_ref, w2_ref, b2_ref,
                    q_out, k_out, kb_out, v_out, vb_out, gq_out, gk_out, gv_out, la_out, gr_out, ga_out, gb_out):
    d = cfg.d_model
    qw, gqk, gvw = cfg.da_w, cfg.gla_qk_w, cfg.gla_v_w
    x = x_ref[...]
    xn = (x * lax.rsqrt(jnp.mean(x * x, axis=-1, keepdims=True) + cfg.eps) * g1_ref[...]).astype(BF16)

    off = [0]

    def proj(width):
        y = _dot(xn, w_ref[:, off[0]:off[0] + width])
        off[0] += width
        return y

    nrep = qw // LANES
    cos = jnp.tile(cos_ref[...], (1, nrep))
    sin = jnp.tile(sin_ref[...], (1, nrep))
    lane = lax.broadcasted_iota(jnp.int32, (x.shape[0], qw), 1)
    first_half = (lane & (cfg.da_head_dim - 1)) < (cfg.da_head_dim // 2)
    seg = seg_ref[...]
    segt = segt_ref[...]

    def qk_norm_rope(t, gain):
        hi, lo = _split_bf16(t * t)
        ss = _dot(hi, seg) + _dot(lo, seg)
        rhi, rlo = _split_bf16(lax.rsqrt(ss * (1.0 / cfg.da_head_dim) + cfg.eps))
        y = t * (_dot(rhi, segt) + _dot(rlo, segt)) * gain
        half = cfg.da_head_dim // 2
        partner = jnp.where(first_half, pltpu.roll(y, qw - half, 1), pltpu.roll(y, half, 1))
        return y * cos + partner * sin

    q = qk_norm_rope(proj(qw), qkg_ref[0:1, :])
    q_out[...] = (q * (cfg.da_head_dim ** -0.5 * math.log2(math.e))).astype(BF16)
    k = qk_norm_rope(proj(qw), qkg_ref[1:2, :])
    for hm in range(qw // cfg.da_head_dim):
        k_out[:, hm // 2, hm % 2, :] = k[:, hm * cfg.da_head_dim:(hm + 1) * cfg.da_head_dim]
    kb_out[...] = k.astype(BF16)
    v = proj(qw)
    v_out[...] = v
    vb_out[...] = v.astype(BF16)
    gq_out[...] = proj(gqk) * (cfg.gla_dk ** -0.5)
    gk_out[...] = proj(gqk)
    gv_out[...] = proj(gvw)
    gr = proj(gvw)
    gr_out[...] = gr * jax.nn.sigmoid(gr)
    ga_out[...] = jax.nn.sigmoid(proj(d))
    gb_out[...] = jax.nn.sigmoid(proj(d))
    glr = proj(LANES)
    gate = _dot(glr.astype(BF16), w2_ref[...]) + b2_ref[...]
    la_out[...] = _log_sigmoid(gate) * (1.0 / cfg.gla_gate_tau)


def _in_proj(cfg, x, cos, sin, p, tm):
    t, d = x.shape
    qw, gqk, gvw = cfg.da_w, cfg.gla_qk_w, cfg.gla_v_w
    n_tab = cos.shape[0] // tm
    tok = lambda w: pl.BlockSpec((tm, w), lambda i: (i, 0))
    const = lambda a: pl.BlockSpec(a.shape, lambda i: (0,) * a.ndim, pipeline_mode=pl.Buffered(1))
    tab = pl.BlockSpec((tm, LANES), lambda i: (i % n_tab, 0))
    consts = [p["g1"], p["w_all"]]
    consts2 = [p["qk_gain"], p["seg"], p["segt"], p["w_gate2"], p["b_gate"]]
    outs = [(qw, BF16), (qw, F32), (qw, BF16), (qw, F32), (qw, BF16), (gqk, F32), (gqk, F32), (gvw, F32),
            (gqk, F32), (gvw, F32), (d, F32), (d, F32)]
    cache_shape = {1: (t, cfg.da_heads, 2, cfg.da_head_dim)}
    cache_spec = {n: pl.BlockSpec((tm,) + s[1:], lambda i, nd=len(s): (i,) + (0,) * (nd - 1))
                  for n, s in cache_shape.items()}
    return pl.pallas_call(
        functools.partial(_in_proj_kernel, cfg),
        grid=(t // tm,),
        in_specs=[tok(d)] + [const(a) for a in consts] + [tab, tab] + [const(a) for a in consts2],
        out_specs=[cache_spec.get(n, tok(w)) for n, (w, _) in enumerate(outs)],
        out_shape=[jax.ShapeDtypeStruct(cache_shape.get(n, (t, w)), dt) for n, (w, dt) in enumerate(outs)],
        compiler_params=_params(1),
        name="in_proj",
    )(x, *consts, cos, sin, *consts2)


def _diff_attn_kernel(cfg, past, tq, tk, lam_init, q_ref, k_ref, v_ref, lamp_ref, g_ref, o_ref,
                      s_ref, mpart_ref, mrow_ref, lpart_ref, acc_ref):
    qi = pl.program_id(2)
    hd, vd = cfg.da_head_dim, cfg.da_v_dim
    shift = int(math.log2(cfg.chunk))
    n_lane_chunks = tk // LANES
    q = q_ref[0]
    lane = lax.broadcasted_iota(jnp.int32, q.shape, 1)
    zero = jnp.zeros_like(q)
    qs = jnp.concatenate([jnp.where(lane < hd, q, zero), jnp.where(lane >= hd, q, zero)], axis=0)

    q_start = past + qi * tq
    n_blocks = (q_start + tq + tk - 1) // tk
    n_full = (q_start + cfg.chunk) // tk

    mpart_ref[...] = jnp.full(mpart_ref.shape, -jnp.inf, F32)

    def scores(j, masked):
        ks = pl.multiple_of(j * tk, tk)
        s = _dot_nt(qs, k_ref[0, pl.ds(ks, tk), :])
        if masked:
            row = lax.broadcasted_iota(jnp.int32, s.shape, 0)
            col = lax.broadcasted_iota(jnp.int32, s.shape, 1)
            qpos = q_start + jnp.where(row >= tq, row - tq, row)
            allowed = lax.shift_right_logical(ks + col, shift) <= lax.shift_right_logical(qpos, shift)
            s = jnp.where(allowed, s, -jnp.inf)
        s_ref[j] = s
        part = s[:, 0:LANES]
        for c in range(1, n_lane_chunks):
            part = jnp.maximum(part, s[:, c * LANES:(c + 1) * LANES])
        mpart_ref[...] = jnp.maximum(mpart_ref[...], part)

    lax.fori_loop(0, n_full, lambda j, c: (scores(j, False), c)[1], 0)
    lax.fori_loop(n_full, n_blocks, lambda j, c: (scores(j, True), c)[1], 0)

    mrow_ref[...] = jnp.broadcast_to(jnp.max(mpart_ref[...], axis=-1, keepdims=True), mrow_ref.shape)
    lpart_ref[...] = jnp.zeros(lpart_ref.shape, F32)
    acc_ref[...] = jnp.zeros(acc_ref.shape, F32)

    def weigh(j, carry):
        ks = pl.multiple_of(j * tk, tk)
        m = mrow_ref[...]
        ps = [jnp.exp2(s_ref[j, :, c * LANES:(c + 1) * LANES] - m) for c in range(n_lane_chunks)]
        lsum = ps[0]
        for pc in ps[1:]:
            lsum = lsum + pc
        lpart_ref[...] += lsum
        p = jnp.concatenate([pc.astype(BF16) for pc in ps], axis=1)
        acc_ref[...] += _dot(p, v_ref[0, pl.ds(ks, tk), :])
        return carry

    lax.fori_loop(0, n_blocks, weigh, 0)

    lp = lamp_ref[...]
    lam = (jnp.exp(jnp.sum(lp[0:1] * lp[1:2], axis=-1, keepdims=True))
           - jnp.exp(jnp.sum(lp[2:3] * lp[3:4], axis=-1, keepdims=True)) + lam_init)
    o = acc_ref[...] / jnp.sum(lpart_ref[...], axis=-1, keepdims=True)
    o = o[:tq] - lam * o[tq:]
    o = o * lax.rsqrt(jnp.mean(o * o, axis=-1, keepdims=True) + cfg.eps) * g_ref[...] * (1.0 - lam_init)
    o_ref[0] = o.astype(BF16)


def _diff_attn(cfg, q, k, v, lam_p, subln_g, past, tq, tk, lam_init):
    b, lq, w = q.shape
    lk = k.shape[1]
    vd = cfg.da_v_dim
    assert lq % tq == 0 and lk % tk == 0 and tk % LANES == 0 and tq % cfg.chunk == 0 and past % cfg.chunk == 0
    assert past + lq <= lk
    kv_spec = pl.BlockSpec((1, lk, vd), lambda bi, h, qi: (bi, 0, h))
    q_spec = pl.BlockSpec((1, tq, vd), lambda bi, h, qi: (bi, qi, h))
    return pl.pallas_call(
        functools.partial(_diff_attn_kernel, cfg, past, tq, tk, lam_init),
        grid=(b, cfg.da_heads, lq // tq),
        in_specs=[q_spec, kv_spec, kv_spec,
                  pl.BlockSpec(lam_p.shape, lambda bi, h, qi: (0, 0)),
                  pl.BlockSpec(subln_g.shape, lambda bi, h, qi: (0, 0))],
        out_specs=q_spec,
        out_shape=jax.ShapeDtypeStruct((b, lq, w), BF16),
        scratch_shapes=[pltpu.VMEM((lk // tk, 2 * tq, tk), F32), pltpu.VMEM((2 * tq, LANES), F32),
                        pltpu.VMEM((2 * tq, LANES), F32), pltpu.VMEM((2 * tq, LANES), F32),
                        pltpu.VMEM((2 * tq, vd), F32)],
        compiler_params=_params(3),
        name="diff_attn",
    )(q, k, v, lam_p, subln_g)


def _gla_constants(c):
    r = np.arange(c)
    i = r[:, None]
    rr = r[None, :]
    mats = [(rr <= i), (rr > i)]
    levels = []
    s = c // 2
    while s >= 1:
        levels.append(s)
        s //= 2
    for s in levels:
        bd = (i // s) * s
        mats.append((rr > bd) & (rr <= i))
        bd2 = (i // s + 1) * s
        mats.append((rr > i) & (rr <= bd2))
    stack = np.concatenate([m.astype(np.float32) for m in mats], axis=0)
    lev = np.full((c, c), -1, np.int32)
    jj = r[None, :]
    for n, s in enumerate(levels):
        sel = (jj < i) & (i // s != jj // s) & (lev < 0)
        lev[sel] = n
    lev[i == jj] = len(levels)
    return stack, lev, len(levels)


def _gla_kernel(cfg, n_levels, n_sub, q_ref, k_ref, v_ref, a_ref, s0_ref, mat_ref, lev_ref, g_ref,
                o_ref, s_out_ref, st_ref):
    c, dk, dv = cfg.chunk, cfg.gla_dk, cfg.gla_dv
    step_i = pl.program_id(1)

    @pl.when(step_i == 0)
    def _():
        for hd in range(cfg.gla_heads):
            st_ref[hd] = s0_ref[0, hd].T

    mat = mat_ref[...]
    lev = lev_ref[...]

    def head_chunk(hd, rows, e_all):
        kcols = slice(hd * dk, (hd + 1) * dk)
        vcols = slice(hd * dv, (hd + 1) * dv)
        q, k, v = q_ref[0, rows, kcols], k_ref[0, rows, kcols], v_ref[0, rows, vcols]
        e = e_all[:, kcols]
        b = e[0:c]
        to_last = e[c:2 * c]
        att = jnp.where(lev == n_levels, _dot_nt(q.astype(BF16), k.astype(BF16)), 0.0)
        for n_l in range(n_levels):
            eq = e[(2 + 2 * n_l) * c:(3 + 2 * n_l) * c]
            ek = e[(3 + 2 * n_l) * c:(4 + 2 * n_l) * c]
            a_l = _dot_nt((q * jnp.exp(eq)).astype(BF16), (k * jnp.exp(ek)).astype(BF16))
            att = jnp.where(lev == n_l, a_l, att)
        st = st_ref[hd]
        vb = v.astype(BF16)
        o = _dot_nt((q * jnp.exp(b)).astype(BF16), st.astype(BF16)) + _dot(att.astype(BF16), vb)
        kl = (k * jnp.exp(to_last)).astype(BF16)
        upd = lax.dot_general(vb, kl, (((0,), (0,)), ((), ())), preferred_element_type=F32)
        st_ref[hd] = st * jnp.exp(b[c - 1:c, :]) + upd
        o = o * lax.rsqrt(jnp.mean(o * o, axis=-1, keepdims=True) + cfg.eps) * g_ref[...]
        o_ref[0, rows, vcols] = o

    def chunk(n, carry):
        rows = pl.ds(pl.multiple_of(n * c, c), c)
        a_hi, a_lo = _split_bf16(a_ref[0, rows, :])
        e_all = _dot(mat, a_hi) + _dot(mat, a_lo)
        for hd in range(cfg.gla_heads):
            head_chunk(hd, rows, e_all)
        return carry

    lax.fori_loop(0, n_sub, chunk, 0)

    @pl.when(step_i == pl.num_programs(1) - 1)
    def _():
        for hd in range(cfg.gla_heads):
            s_out_ref[0, hd] = st_ref[hd].T


def _gla(cfg, gq, gk, gv, la, s0, norm_g, lb):
    b, l, _ = gq.shape
    c, dk, dv, h = cfg.chunk, cfg.gla_dk, cfg.gla_dv, cfg.gla_heads
    assert l % lb == 0 and lb % c == 0
    stack, lev, n_levels = _gla_constants(c)
    mat = jnp.asarray(stack, BF16)
    lev = jnp.asarray(lev)
    seq = lambda w: pl.BlockSpec((1, lb, h * w), lambda bi, si: (bi, si, 0))
    state = pl.BlockSpec((1, h, dk, dv), lambda bi, si: (bi, 0, 0, 0))
    const = lambda a: pl.BlockSpec(a.shape, lambda bi, si: (0,) * a.ndim)
    return pl.pallas_call(
        functools.partial(_gla_kernel, cfg, n_levels, lb // c),
        grid=(b, l // lb),
        in_specs=[seq(dk), seq(dk), seq(dv), seq(dk), state, const(mat), const(lev), const(norm_g)],
        out_specs=[seq(dv), state],
        out_shape=[jax.ShapeDtypeStruct((b, l, h * dv), F32), jax.ShapeDtypeStruct((b, h, dk, dv), F32)],
        scratch_shapes=[pltpu.VMEM((h, dv, dk), F32)],
        compiler_params=_params(2, arbitrary_last=True),
        name="gla",
    )(gq, gk, gv, la, s0, mat, lev, norm_g)


def _merge_kernel(cfg, x_ref, oa_ref, ob_ref, gr_ref, ga_ref, gb_ref, wa_ref, wb_ref, wo_ref, g2_ref,
                  h_ref, hn_ref):
    a = _dot(oa_ref[...], wa_ref[...])
    bb = _dot((ob_ref[...] * gr_ref[...]).astype(BF16), wb_ref[...])
    mix = ga_ref[...] * a + gb_ref[...] * bb
    h = x_ref[...] + _dot(mix.astype(BF16), wo_ref[...])
    h_ref[...] = h
    hn = h * lax.rsqrt(jnp.mean(h * h, axis=-1, keepdims=True) + cfg.eps) * g2_ref[...]
    hn_ref[...] = hn.astype(BF16)


def _merge_out(cfg, x, oa, ob, gr, ga, gb, p, tm):
    t, d = x.shape
    tok = lambda a: pl.BlockSpec((tm, a.shape[1]), lambda i: (i, 0))
    const = lambda a: pl.BlockSpec(a.shape, lambda i: (0,) * a.ndim, pipeline_mode=pl.Buffered(1))
    toks = [x, oa, ob, gr, ga, gb]
    consts = [p["w_a"], p["w_b"], p["w_o"], p["g2"]]
    return pl.pallas_call(
        functools.partial(_merge_kernel, cfg),
        grid=(t // tm,),
        in_specs=[tok(a) for a in toks] + [const(a) for a in consts],
        out_specs=[pl.BlockSpec((tm, d), lambda i: (i, 0))] * 2,
        out_shape=[jax.ShapeDtypeStruct((t, d), F32), jax.ShapeDtypeStruct((t, d), BF16)],
        compiler_params=_params(1),
        name="merge_out",
    )(*toks, *consts)


def _top_values(x, k):
    vals = []
    rank = jnp.full(x.shape, float(k), F32)
    for n in range(k):
        m = jnp.max(x, axis=0, keepdims=True)
        vals.append(m)
        hit = x == m
        rank = jnp.where(hit, float(n), rank)
        x = jnp.where(hit, -jnp.inf, x)
    return vals, rank


def _rows_to_block(rows):
    k, t = len(rows), rows[0].shape[1]
    idx = lax.broadcasted_iota(jnp.int32, (k, t), 0)
    out = jnp.zeros((k, t), F32)
    for n, r in enumerate(rows):
        out = jnp.where(idx == n, r, out)
    return out


def _candidate_sums(top0, top1, k):
    tile = 8
    assert k % tile == 0 and k // (tile + 1) == 1
    blk0, blk1 = _rows_to_block(top0), _rows_to_block(top1)
    row = lax.broadcasted_iota(jnp.int32, (tile, top0[0].shape[1]), 0)
    pieces = []
    for b in range(tile):
        n_a = k // (b + 1)
        for a0 in range(0, n_a, tile):
            piece = blk0[a0:a0 + tile] + top1[b]
            pieces.append(piece if a0 + tile <= n_a else jnp.where(row < n_a - a0, piece, -jnp.inf))
    for b0 in range(tile, k, tile):
        pieces.append(blk1[b0:b0 + tile] + top0[0])
    return jnp.concatenate(pieces, axis=0)


def _peer_route_kernel(cfg, hn_ref, wq_ref, keys_ref, c1_ref, r2_ref, e1_ref, e2_ref):
    nk, topk = cfg.peer_n_keys, cfg.peer_topk
    sub = cfg.peer_qdim // 2
    q = _dot(hn_ref[...], wq_ref[...]).astype(BF16)
    for h in range(cfg.peer_heads):
        sc, top, rank = [], [], []
        for half in range(2):
            col = (2 * h + half) * sub
            s = _dot_nt(keys_ref[2 * h + half], q[:, col:col + sub])
            vals, rk = _top_values(s, topk)
            sc.append(s)
            top.append(vals)
            rank.append(rk)
        best, _ = _top_values(_candidate_sums(top[0], top[1], topk), topk)
        tau = best[topk - 1]
        z = best[0] * 0.0
        for cv in best:
            z = z + jnp.exp(cv - best[0])
        count = jnp.zeros(sc[0].shape, F32)
        for b in range(topk):
            count = count + jnp.where(sc[0] + top[1][b] >= tau, 1.0, 0.0)
        rows = pl.ds(h * nk, nk)
        c1_ref[rows, :] = count
        r2_ref[rows, :] = rank[1]
        e1_ref[rows, :] = jnp.exp(sc[0] - top[0][0]) / z
        e2_ref[rows, :] = jnp.exp(sc[1] - top[1][0])


def _peer_route(cfg, hn, wq, keys, tb):
    t, d = hn.shape
    hk = cfg.peer_heads * cfg.peer_n_keys
    const = lambda a: pl.BlockSpec(a.shape, lambda i: (0,) * a.ndim, pipeline_mode=pl.Buffered(1))
    col = lambda r: pl.BlockSpec((r, tb), lambda i: (0, i))
    return pl.pallas_call(
        functools.partial(_peer_route_kernel, cfg),
        grid=(t // tb,),
        in_specs=[pl.BlockSpec((tb, d), lambda i: (i, 0)), const(wq), const(keys)],
        out_specs=[col(hk)] * 4,
        out_shape=[jax.ShapeDtypeStruct((hk, t), F32)] * 4,
        compiler_params=_params(1),
        name="peer_route",
    )(hn, wq, keys)


def _gelu(x):
    return 0.5 * x * (1.0 + lax.erf(x * (2.0 ** -0.5)))


def _peer_dense_kernel(cfg, ib, hn_ref, h_ref, u_ref, vt_ref, c1_ref, r2_ref, e1_ref, e2_ref,
                       y_ref, acc_ref, act_ref, p_ref):
    nk = cfg.peer_n_keys
    ei = pl.program_id(1)
    tb = hn_ref.shape[0]

    @pl.when(ei == 0)
    def _():
        acc_ref[...] = jnp.zeros(acc_ref.shape, F32)

    act_ref[...] = _gelu(_dot_nt(u_ref[...], hn_ref[...]))

    n_row_tiles = nk // GATE_ROWS

    def gate_tile(n, carry):
        r0 = pl.multiple_of((n % n_row_tiles) * GATE_ROWS, GATE_ROWS)
        lanes = pl.ds(pl.multiple_of((n // n_row_tiles) * LANES, LANES), LANES)
        for i0 in range(0, ib, GATE_FIRST):
            g = [None] * GATE_FIRST
            for h in range(cfg.peer_heads):
                rows = pl.ds(h * nk + r0, GATE_ROWS)
                r2, e2 = r2_ref[rows, lanes], e2_ref[rows, lanes]
                for k in range(GATE_FIRST):
                    keep = r2 < c1_ref[h, i0 + k:i0 + k + 1, lanes]
                    term = jnp.where(keep, e2, 0.0) * e1_ref[h, i0 + k:i0 + k + 1, lanes]
                    g[k] = term if h == 0 else g[k] + term
            for k in range(GATE_FIRST):
                rows = pl.ds((i0 + k) * nk + r0, GATE_ROWS)
                p_ref[rows, lanes] = (g[k] * act_ref[rows, lanes]).astype(BF16)
        return carry

    lax.fori_loop(0, n_row_tiles * (tb // LANES), gate_tile, 0)
    acc_ref[...] += _dot(vt_ref[...], p_ref[...])

    @pl.when(ei == pl.num_programs(1) - 1)
    def _():
        y_ref[...] = h_ref[...] + acc_ref[...].T


def _peer_dense(cfg, hn, h, u, vt, route, tb, ib):
    t, d = hn.shape
    nk = cfg.peer_n_keys
    eb = ib * nk
    hk = cfg.peer_heads * nk
    c1, r2, e1, e2 = route
    by_head = lambda a: a.reshape(cfg.peer_heads, nk, t)
    tokb = pl.BlockSpec((tb, d), lambda ti, ei: (ti, 0))
    col = lambda r: pl.BlockSpec((r, tb), lambda ti, ei: (0, ti))
    first = pl.BlockSpec((cfg.peer_heads, ib, tb), lambda ti, ei: (0, ei, ti))
    return pl.pallas_call(
        functools.partial(_peer_dense_kernel, cfg, ib),
        grid=(t // tb, cfg.n_experts // eb),
        in_specs=[tokb, tokb,
                  pl.BlockSpec((eb, d), lambda ti, ei: (ei, 0)),
                  pl.BlockSpec((d, eb), lambda ti, ei: (0, ei)),
                  first, col(hk), first, col(hk)],
        out_specs=tokb,
        out_shape=jax.ShapeDtypeStruct((t, d), F32),
        scratch_shapes=[pltpu.VMEM((d, tb), F32), pltpu.VMEM((eb, tb), F32), pltpu.VMEM((eb, tb), BF16)],
        compiler_params=_params(2, arbitrary_last=True),
        name="peer_dense",
    )(hn, h, u, vt, by_head(c1), r2, by_head(e1), e2)


def _prep_params(cfg, norm1_g, w_in, qk_norm_g, da_lambda, da_subln_g, gla_w_gate2, gla_b_gate, gla_norm_g,
                 w_branch_a, w_branch_b, w_out, norm2_g, peer_w_q, peer_sub_keys, peer_u, peer_v):
    d, qw, gqk, gvw, rank = cfg.d_model, cfg.da_w, cfg.gla_qk_w, cfg.gla_v_w, cfg.gla_gate_rank
    splits = (qw, qw, qw, gqk, gqk, gvw, rank, gvw, d, d)
    starts = np.concatenate([[0], np.cumsum(splits)])
    piece = lambda n: w_in[:, starts[n]:starts[n + 1]]
    glr = jnp.pad(piece(6), ((0, 0), (0, LANES - rank)))
    w_all = jnp.concatenate([piece(n) for n in (0, 1, 2, 3, 4, 5, 7, 8, 9)] + [glr], axis=1).astype(BF16)
    group = np.arange(qw) // cfg.da_head_dim
    seg = (group[:, None] == np.arange(LANES)[None, :]).astype(np.float32)
    return {
        "g1": norm1_g[None, :], "w_all": w_all,
        "qk_gain": jnp.tile(qk_norm_g, (1, qw // cfg.da_head_dim)),
        "seg": jnp.asarray(seg, BF16), "segt": jnp.asarray(seg.T, BF16),
        "w_gate2": jnp.pad(gla_w_gate2, ((0, LANES - rank), (0, 0))).astype(BF16),
        "b_gate": gla_b_gate[None, :],
        "da_lambda": da_lambda, "subln_g": da_subln_g[None, :], "gla_norm_g": gla_norm_g[None, :],
        "w_a": w_branch_a.astype(BF16), "w_b": w_branch_b.astype(BF16), "w_o": w_out.astype(BF16),
        "g2": norm2_g[None, :],
        "wq": peer_w_q.astype(BF16),
        "keys": peer_sub_keys.reshape(cfg.peer_heads * 2, cfg.peer_n_keys, cfg.peer_qdim // 2).astype(BF16),
        "u": peer_u.astype(BF16), "vt": peer_v.astype(BF16).T,
    }


def _pick(n, target):
    t = min(n, target)
    while n % t:
        t //= 2
    return t


def _layer(cfg, x, cache_k, cache_v, s0, p, lam_init):
    b, l, d = x.shape
    t = b * l
    past = 0 if cache_k is None else cache_k.shape[1]
    qw = cfg.da_w

    tm = _pick(t, 256)
    if l % tm == 0:
        positions = past + jnp.arange(l)
    else:
        positions = past + (jnp.arange(tm) % l)
    cos, sin = _rope_tables(cfg, positions)
    q, k, kb, v, vb, gq, gk, gv, la, gr, ga, gb = _in_proj(cfg, x.reshape(t, d), cos, sin, p, tm)

    r3 = lambda a: a.reshape(b, l, a.shape[-1])
    if cache_k is None:
        kk, vv = r3(kb), r3(vb)
        tq = _pick(l, ATTN_TQ)
        tk = _pick(l, ATTN_TK)
    else:
        tq = _pick(l, ATTN_TQ)
        tk = -(-(past + l) // LANES) * LANES
        pad = jnp.zeros((b, tk - past - l, qw), BF16)
        kk = jnp.concatenate([cache_k.reshape(b, past, qw).astype(BF16), r3(kb), pad], axis=1)
        vv = jnp.concatenate([cache_v.reshape(b, past, qw).astype(BF16), r3(vb), pad], axis=1)
    oa = _diff_attn(cfg, r3(q), kk, vv, p["da_lambda"], p["subln_g"], past, tq, tk, lam_init)

    if s0 is None:
        s0 = jnp.zeros((b, cfg.gla_heads, cfg.gla_dk, cfg.gla_dv), F32)
    ob, s_new = _gla(cfg, r3(gq), r3(gk), r3(gv), r3(la), s0, p["gla_norm_g"], _pick(l, 512))

    h, hn = _merge_out(cfg, x.reshape(t, d), oa.reshape(t, qw), ob.reshape(t, -1), gr, ga, gb, p, tm)

    tb = _pick(t, 512)
    route = _peer_route(cfg, hn, p["wq"], p["keys"], _pick(t, ROUTE_TOKENS))
    y = _peer_dense(cfg, hn, h, p["u"], p["vt"], route, tb, PEER_FIRST_PER_STEP)

    k_new = k.reshape(b, l, cfg.da_heads, 2, cfg.da_head_dim)
    v_new = v.reshape(b, l, cfg.da_heads, cfg.da_v_dim)
    return y.reshape(b, l, d), k_new, v_new, s_new


def _forward(cfg, x_prompt, x_sample, cache_k, cache_v, state_gla, *weights):
    yp, ys = x_prompt, x_sample
    outs = [[] for _ in range(6)]
    for layer in range(cache_k.shape[0]):
        p = _prep_params(cfg, *[w[layer] for w in weights])
        lam_init = 0.8 - 0.6 * math.exp(-0.3 * layer)
        yp, kp, vp, sp = _layer(cfg, yp, None, None, None, p, lam_init)
        ys, kn, vn, sn = _layer(cfg, ys, cache_k[layer], cache_v[layer], state_gla[layer], p, lam_init)
        for lst, a in zip(outs, (kp, vp, sp, kn, vn, sn)):
            lst.append(a)
    return (yp, ys) + tuple(jnp.stack(o) for o in outs)


def kernel(x_prompt, x_sample, cache_k, cache_v, state_gla, norm1_g, w_in, qk_norm_g, da_lambda, da_subln_g,
           gla_w_gate2, gla_b_gate, gla_norm_g, w_branch_a, w_branch_b, w_out, norm2_g, peer_w_q,
           peer_sub_keys, peer_u, peer_v):
    return _forward(Cfg(), x_prompt, x_sample, cache_k, cache_v, state_gla, norm1_g, w_in, qk_norm_g, da_lambda,
                    da_subln_g, gla_w_gate2, gla_b_gate, gla_norm_g, w_branch_a, w_branch_b, w_out, norm2_g,
                    peer_w_q, peer_sub_keys, peer_u, peer_v)
```

```python
import functools
import math
from typing import NamedTuple

import jax
import jax.numpy as jnp
import numpy as np
from jax import lax
from jax.experimental import pallas as pl
from jax.experimental.pallas import tpu as pltpu

F32 = jnp.float32
BF16 = jnp.bfloat16
LANES = 128
VMEM_LIMIT = 56 * 1024 * 1024
GATE_ROWS = 16
ROUTE_TOKENS = 256
PEER_FIRST_PER_STEP = 16
GATE_FIRST = 8
ATTN_TQ = 512
ATTN_TK = 512


class Cfg(NamedTuple):
    d_model: int = 1024
    chunk: int = 64
    eps: float = 1e-6
    da_heads: int = 8
    da_head_dim: int = 64
    rope_theta: float = 10000.0
    gla_heads: int = 4
    gla_gate_rank: int = 16
    gla_gate_tau: float = 16.0
    peer_heads: int = 8
    peer_n_keys: int = 128
    peer_qdim: int = 256
    peer_topk: int = 16

    @property
    def da_v_dim(self): return 2 * self.da_head_dim
    @property
    def da_w(self): return self.da_heads * 2 * self.da_head_dim
    @property
    def gla_dk(self): return self.d_model // 2 // self.gla_heads
    @property
    def gla_dv(self): return self.d_model // self.gla_heads
    @property
    def gla_qk_w(self): return self.gla_heads * self.gla_dk
    @property
    def gla_v_w(self): return self.gla_heads * self.gla_dv
    @property
    def n_experts(self): return self.peer_n_keys * self.peer_n_keys


def _dot(a, b):
    return jnp.dot(a, b, preferred_element_type=F32)


def _dot_nt(a, b):
    return lax.dot_general(a, b, (((1,), (1,)), ((), ())), preferred_element_type=F32)


def _split_bf16(t):
    hi = t.astype(BF16)
    lo = (t - hi.astype(F32)).astype(BF16)
    return hi, lo


def _params(n_axes, arbitrary_last=False):
    sem = ["parallel"] * n_axes
    if arbitrary_last:
        sem[-1] = "arbitrary"
    return pltpu.CompilerParams(dimension_semantics=tuple(sem), vmem_limit_bytes=VMEM_LIMIT)


def _rope_table_kernel(pos_ref, freq_ref, sign_ref, cos_ref, sin_ref):
    ang = pos_ref[...] * freq_ref[...]
    cos_ref[...] = jnp.cos(ang)
    sin_ref[...] = jnp.sin(ang) * sign_ref[...]


def _rope_tables(cfg, positions):
    half = cfg.da_head_dim // 2
    r = positions.shape[0]
    lane = np.arange(LANES)
    inv_freq = cfg.rope_theta ** (-jnp.arange(half, dtype=F32) / half)
    freq = jnp.tile(inv_freq, LANES // half)[None, :]
    sign = jnp.asarray(np.where(lane % cfg.da_head_dim < half, -1.0, 1.0), F32)[None, :]
    pos = jnp.broadcast_to(positions.astype(F32)[:, None], (r, LANES))
    full = lambda shape: pl.BlockSpec(shape, lambda: (0, 0))
    return pl.pallas_call(
        _rope_table_kernel,
        out_shape=(jax.ShapeDtypeStruct((r, LANES), F32),) * 2,
        in_specs=[full((r, LANES)), full((1, LANES)), full((1, LANES))],
        out_specs=(full((r, LANES)),) * 2,
        name="rope_table",
    )(pos, freq, sign)


def _log_sigmoid(x):
    return jnp.minimum(x, 0.0) - jnp.log1p(jnp.exp(-jnp.abs(x)))


def _in_proj_kernel(cfg, x_ref, g1_ref, w_ref, cos_ref, sin_ref, qkg_ref, seg_ref, w2_ref, b2_ref,
                    q_out, k_out, kb_out, v_out, vb_out, gq_out, gk_out, gv_out, la_out, gr_out, ga_out, gb_out):
    d = cfg.d_model
    qw, gqk, gvw = cfg.da_w, cfg.gla_qk_w, cfg.gla_v_w
    x = x_ref[...]
    xn = (x * lax.rsqrt(jnp.mean(x * x, axis=-1, keepdims=True) + cfg.eps) * g1_ref[...]).astype(BF16)

    widths = (("q", qw), ("k", qw), ("v", qw), ("gq", gqk), ("gk", gqk), ("gv", gvw), ("gr", gvw),
              ("ga", d), ("gb", d), ("glr", LANES))
    cols, start = {}, 0
    for name, width in widths:
        cols[name] = slice(start, start + width)
        start += width

    def proj(name):
        return _dot(xn, w_ref[:, cols[name]])

    nrep = qw // LANES
    cos = jnp.tile(cos_ref[...], (1, nrep))
    sin = jnp.tile(sin_ref[...], (1, nrep))
    lane = lax.broadcasted_iota(jnp.int32, (x.shape[0], qw), 1)
    first_half = (lane & (cfg.da_head_dim - 1)) < (cfg.da_head_dim // 2)
    seg = seg_ref[...]
    lane128 = lax.broadcasted_iota(jnp.int32, (x.shape[0], LANES), 1)

    def group_sums(t):
        hi, lo = _split_bf16(t * t)
        return _dot(hi, seg) + _dot(lo, seg)

    def qk_norm_rope(t, ss, gain):
        r = lax.rsqrt(ss * (1.0 / cfg.da_head_dim) + cfg.eps)
        per_tile = LANES // cfg.da_head_dim
        spread = []
        for v in range(nrep):
            tile = r[:, v * per_tile:v * per_tile + 1]
            for g in range(1, per_tile):
                col = v * per_tile + g
                tile = jnp.where(lane128 < g * cfg.da_head_dim, tile, r[:, col:col + 1])
            spread.append(jnp.broadcast_to(tile, (t.shape[0], LANES)))
        y = t * jnp.concatenate(spread, axis=1) * gain
        half = cfg.da_head_dim // 2
        partner = jnp.where(first_half, pltpu.roll(y, qw - half, 1), pltpu.roll(y, half, 1))
        return y * cos + partner * sin

    q_raw = proj("q")
    q = qk_norm_rope(q_raw, group_sums(q_raw), qkg_ref[0:1, :])
    q_out[...] = (q * (cfg.da_head_dim ** -0.5 * math.log2(math.e))).astype(BF16)
    k_raw = proj("k")
    k = qk_norm_rope(k_raw, group_sums(k_raw), qkg_ref[1:2, :])
    for hm in range(qw // cfg.da_head_dim):
        k_out[:, hm // 2, hm % 2, :] = k[:, hm * cfg.da_head_dim:(hm + 1) * cfg.da_head_dim]
    kb_out[...] = k.astype(BF16)
    v = proj("v")
    v_out[...] = v
    vb_out[...] = v.astype(BF16)
    gq_out[...] = proj("gq") * (cfg.gla_dk ** -0.5)
    gk_out[...] = proj("gk")
    gv_out[...] = proj("gv")
    gr = proj("gr")
    gr_out[...] = gr * jax.nn.sigmoid(gr)
    ga_out[...] = jax.nn.sigmoid(proj("ga"))
    gb_out[...] = jax.nn.sigmoid(proj("gb"))
    gate = _dot(proj("glr").astype(BF16), w2_ref[...]) + b2_ref[...]
    la_out[...] = _log_sigmoid(gate) * (1.0 / cfg.gla_gate_tau)


def _in_proj(cfg, x, cos, sin, p, tm):
    t, d = x.shape
    qw, gqk, gvw = cfg.da_w, cfg.gla_qk_w, cfg.gla_v_w
    n_tab = cos.shape[0] // tm
    tok = lambda w: pl.BlockSpec((tm, w), lambda i: (i, 0))
    const = lambda a: pl.BlockSpec(a.shape, lambda i: (0,) * a.ndim, pipeline_mode=pl.Buffered(1))
    tab = pl.BlockSpec((tm, LANES), lambda i: (i % n_tab, 0))
    consts = [p["g1"], p["w_all"]]
    consts2 = [p["qk_gain"], p["seg"], p["w_gate2"], p["b_gate"]]
    outs = [(qw, BF16), (qw, F32), (qw, BF16), (qw, F32), (qw, BF16), (gqk, F32), (gqk, F32), (gvw, F32),
            (gqk, F32), (gvw, F32), (d, F32), (d, F32)]
    cache_shape = {1: (t, cfg.da_heads, 2, cfg.da_head_dim)}
    cache_spec = {n: pl.BlockSpec((tm,) + s[1:], lambda i, nd=len(s): (i,) + (0,) * (nd - 1))
                  for n, s in cache_shape.items()}
    return pl.pallas_call(
        functools.partial(_in_proj_kernel, cfg),
        grid=(t // tm,),
        in_specs=[tok(d)] + [const(a) for a in consts] + [tab, tab] + [const(a) for a in consts2],
        out_specs=[cache_spec.get(n, tok(w)) for n, (w, _) in enumerate(outs)],
        out_shape=[jax.ShapeDtypeStruct(cache_shape.get(n, (t, w)), dt) for n, (w, dt) in enumerate(outs)],
        compiler_params=_params(1),
        name="in_proj",
    )(x, *consts, cos, sin, *consts2)


def _diff_attn_kernel(cfg, past, tq, tk, lam_init, q_ref, k_ref, v_ref, lamp_ref, g_ref, o_ref,
                      s_ref, mpart_ref, mrow_ref, lpart_ref, acc_ref):
    qi = pl.program_id(2)
    hd, vd = cfg.da_head_dim, cfg.da_v_dim
    shift = int(math.log2(cfg.chunk))
    n_lane_chunks = tk // LANES
    q = q_ref[0]
    lane = lax.broadcasted_iota(jnp.int32, q.shape, 1)
    zero = jnp.zeros_like(q)
    qs = jnp.concatenate([jnp.where(lane < hd, q, zero), jnp.where(lane >= hd, q, zero)], axis=0)

    q_start = past + qi * tq
    n_blocks = (q_start + tq + tk - 1) // tk
    n_full = (q_start + cfg.chunk) // tk

    mpart_ref[...] = jnp.full(mpart_ref.shape, -jnp.inf, F32)

    def scores(j, masked):
        ks = pl.multiple_of(j * tk, tk)
        s = _dot_nt(qs, k_ref[0, pl.ds(ks, tk), :])
        if masked:
            row = lax.broadcasted_iota(jnp.int32, s.shape, 0)
            col = lax.broadcasted_iota(jnp.int32, s.shape, 1)
            qpos = q_start + jnp.where(row >= tq, row - tq, row)
            allowed = lax.shift_right_logical(ks + col, shift) <= lax.shift_right_logical(qpos, shift)
            s = jnp.where(allowed, s, -jnp.inf)
        s_ref[j] = s
        part = s[:, 0:LANES]
        for c in range(1, n_lane_chunks):
            part = jnp.maximum(part, s[:, c * LANES:(c + 1) * LANES])
        mpart_ref[...] = jnp.maximum(mpart_ref[...], part)

    lax.fori_loop(0, n_full, lambda j, c: (scores(j, False), c)[1], 0)
    lax.fori_loop(n_full, n_blocks, lambda j, c: (scores(j, True), c)[1], 0)

    mrow_ref[...] = jnp.broadcast_to(jnp.max(mpart_ref[...], axis=-1, keepdims=True), mrow_ref.shape)
    lpart_ref[...] = jnp.zeros(lpart_ref.shape, F32)
    acc_ref[...] = jnp.zeros(acc_ref.shape, F32)

    def weigh(j, carry):
        ks = pl.multiple_of(j * tk, tk)
        m = mrow_ref[...]
        ps = [jnp.exp2(s_ref[j, :, c * LANES:(c + 1) * LANES] - m) for c in range(n_lane_chunks)]
        lsum = ps[0]
        for pc in ps[1:]:
            lsum = lsum + pc
        lpart_ref[...] += lsum
        p = jnp.concatenate([pc.astype(BF16) for pc in ps], axis=1)
        acc_ref[...] += _dot(p, v_ref[0, pl.ds(ks, tk), :])
        return carry

    lax.fori_loop(0, n_blocks, weigh, 0)

    lp = lamp_ref[...]
    lam = (jnp.exp(jnp.sum(lp[0:1] * lp[1:2], axis=-1, keepdims=True))
           - jnp.exp(jnp.sum(lp[2:3] * lp[3:4], axis=-1, keepdims=True)) + lam_init)
    o = acc_ref[...] / jnp.sum(lpart_ref[...], axis=-1, keepdims=True)
    o = o[:tq] - lam * o[tq:]
    o = o * lax.rsqrt(jnp.mean(o * o, axis=-1, keepdims=True) + cfg.eps) * g_ref[...] * (1.0 - lam_init)
    o_ref[0] = o.astype(BF16)


def _diff_attn(cfg, q, k, v, lam_p, subln_g, past, tq, tk, lam_init):
    b, lq, w = q.shape
    lk = k.shape[1]
    vd = cfg.da_v_dim
    assert lq % tq == 0 and lk % tk == 0 and tk % LANES == 0 and tq % cfg.chunk == 0 and past % cfg.chunk == 0
    assert past + lq <= lk
    kv_spec = pl.BlockSpec((1, lk, vd), lambda bi, h, qi: (bi, 0, h))
    q_spec = pl.BlockSpec((1, tq, vd), lambda bi, h, qi: (bi, qi, h))
    return pl.pallas_call(
        functools.partial(_diff_attn_kernel, cfg, past, tq, tk, lam_init),
        grid=(b, cfg.da_heads, lq // tq),
        in_specs=[q_spec, kv_spec, kv_spec,
                  pl.BlockSpec(lam_p.shape, lambda bi, h, qi: (0, 0)),
                  pl.BlockSpec(subln_g.shape, lambda bi, h, qi: (0, 0))],
        out_specs=q_spec,
        out_shape=jax.ShapeDtypeStruct((b, lq, w), BF16),
        scratch_shapes=[pltpu.VMEM((lk // tk, 2 * tq, tk), F32), pltpu.VMEM((2 * tq, LANES), F32),
                        pltpu.VMEM((2 * tq, LANES), F32), pltpu.VMEM((2 * tq, LANES), F32),
                        pltpu.VMEM((2 * tq, vd), F32)],
        compiler_params=_params(3),
        name="diff_attn",
    )(q, k, v, lam_p, subln_g)


def _gla_constants(c):
    r = np.arange(c)
    i = r[:, None]
    rr = r[None, :]
    mats = [(rr <= i), (rr > i)]
    levels = []
    s = c // 2
    while s >= 1:
        levels.append(s)
        s //= 2
    for s in levels:
        bd = (i // s) * s
        mats.append((rr > bd) & (rr <= i))
        bd2 = (i // s + 1) * s
        mats.append((rr > i) & (rr <= bd2))
    stack = np.concatenate([m.astype(np.float32) for m in mats], axis=0)
    lev = np.full((c, c), -1, np.int32)
    jj = r[None, :]
    for n, s in enumerate(levels):
        sel = (jj < i) & (i // s != jj // s) & (lev < 0)
        lev[sel] = n
    lev[i == jj] = len(levels)
    return stack, lev, len(levels)


def _gla_kernel(cfg, n_levels, n_sub, q_ref, k_ref, v_ref, a_ref, s0_ref, mat_ref, lev_ref, g_ref,
                o_ref, s_out_ref, st_ref):
    c, dk, dv = cfg.chunk, cfg.gla_dk, cfg.gla_dv
    step_i = pl.program_id(1)

    @pl.when(step_i == 0)
    def _():
        for hd in range(cfg.gla_heads):
            st_ref[hd] = s0_ref[0, hd].T

    mat = mat_ref[...]
    lev = lev_ref[...]

    heads = range(cfg.gla_heads)
    kcols = [slice(hd * dk, (hd + 1) * dk) for hd in heads]
    vcols = [slice(hd * dv, (hd + 1) * dv) for hd in heads]

    def chunk(n, carry):
        rows = pl.ds(pl.multiple_of(n * c, c), c)
        a_hi, a_lo = _split_bf16(a_ref[0, rows, :])
        e_all = _dot(mat, a_hi) + _dot(mat, a_lo)
        e = [e_all[:, kcols[hd]] for hd in heads]
        q = [q_ref[0, rows, kcols[hd]] for hd in heads]
        k = [k_ref[0, rows, kcols[hd]] for hd in heads]
        vb = [v_ref[0, rows, vcols[hd]].astype(BF16) for hd in heads]

        att = [jnp.where(lev == n_levels, _dot_nt(q[hd].astype(BF16), k[hd].astype(BF16)), 0.0) for hd in heads]
        for n_l in range(n_levels):
            eq = slice((2 + 2 * n_l) * c, (3 + 2 * n_l) * c)
            ek = slice((3 + 2 * n_l) * c, (4 + 2 * n_l) * c)
            a_l = [_dot_nt((q[hd] * jnp.exp(e[hd][eq])).astype(BF16), (k[hd] * jnp.exp(e[hd][ek])).astype(BF16))
                   for hd in heads]
            att = [jnp.where(lev == n_l, a_l[hd], att[hd]) for hd in heads]

        st = [st_ref[hd] for hd in heads]
        o = [_dot_nt((q[hd] * jnp.exp(e[hd][0:c])).astype(BF16), st[hd].astype(BF16))
             + _dot(att[hd].astype(BF16), vb[hd]) for hd in heads]
        kl = [(k[hd] * jnp.exp(e[hd][c:2 * c])).astype(BF16) for hd in heads]
        upd = [lax.dot_general(vb[hd], kl[hd], (((0,), (0,)), ((), ())), preferred_element_type=F32)
               for hd in heads]
        for hd in heads:
            st_ref[hd] = st[hd] * jnp.exp(e[hd][c - 1:c, :]) + upd[hd]
            on = o[hd] * lax.rsqrt(jnp.mean(o[hd] * o[hd], axis=-1, keepdims=True) + cfg.eps) * g_ref[...]
            o_ref[0, rows, vcols[hd]] = on
        return carry

    lax.fori_loop(0, n_sub, chunk, 0, unroll=2 if n_sub % 2 == 0 else 1)

    @pl.when(step_i == pl.num_programs(1) - 1)
    def _():
        for hd in range(cfg.gla_heads):
            s_out_ref[0, hd] = st_ref[hd].T


def _gla(cfg, gq, gk, gv, la, s0, norm_g, lb):
    b, l, _ = gq.shape
    c, dk, dv, h = cfg.chunk, cfg.gla_dk, cfg.gla_dv, cfg.gla_heads
    assert l % lb == 0 and lb % c == 0
    stack, lev, n_levels = _gla_constants(c)
    mat = jnp.asarray(stack, BF16)
    lev = jnp.asarray(lev)
    seq = lambda w: pl.BlockSpec((1, lb, h * w), lambda bi, si: (bi, si, 0))
    state = pl.BlockSpec((1, h, dk, dv), lambda bi, si: (bi, 0, 0, 0))
    const = lambda a: pl.BlockSpec(a.shape, lambda bi, si: (0,) * a.ndim)
    return pl.pallas_call(
        functools.partial(_gla_kernel, cfg, n_levels, lb // c),
        grid=(b, l // lb),
        in_specs=[seq(dk), seq(dk), seq(dv), seq(dk), state, const(mat), const(lev), const(norm_g)],
        out_specs=[seq(dv), state],
        out_shape=[jax.ShapeDtypeStruct((b, l, h * dv), F32), jax.ShapeDtypeStruct((b, h, dk, dv), F32)],
        scratch_shapes=[pltpu.VMEM((h, dv, dk), F32)],
        compiler_params=_params(2, arbitrary_last=True),
        name="gla",
    )(gq, gk, gv, la, s0, mat, lev, norm_g)


def _merge_kernel(cfg, x_ref, oa_ref, ob_ref, gr_ref, ga_ref, gb_ref, wa_ref, wb_ref, wo_ref, g2_ref,
                  h_ref, hn_ref):
    a = _dot(oa_ref[...], wa_ref[...])
    bb = _dot((ob_ref[...] * gr_ref[...]).astype(BF16), wb_ref[...])
    mix = ga_ref[...] * a + gb_ref[...] * bb
    h = x_ref[...] + _dot(mix.astype(BF16), wo_ref[...])
    h_ref[...] = h
    hn = h * lax.rsqrt(jnp.mean(h * h, axis=-1, keepdims=True) + cfg.eps) * g2_ref[...]
    hn_ref[...] = hn.astype(BF16)


def _merge_out(cfg, x, oa, ob, gr, ga, gb, p, tm):
    t, d = x.shape
    tok = lambda a: pl.BlockSpec((tm, a.shape[1]), lambda i: (i, 0))
    const = lambda a: pl.BlockSpec(a.shape, lambda i: (0,) * a.ndim, pipeline_mode=pl.Buffered(1))
    toks = [x, oa, ob, gr, ga, gb]
    consts = [p["w_a"], p["w_b"], p["w_o"], p["g2"]]
    return pl.pallas_call(
        functools.partial(_merge_kernel, cfg),
        grid=(t // tm,),
        in_specs=[tok(a) for a in toks] + [const(a) for a in consts],
        out_specs=[pl.BlockSpec((tm, d), lambda i: (i, 0))] * 2,
        out_shape=[jax.ShapeDtypeStruct((t, d), F32), jax.ShapeDtypeStruct((t, d), BF16)],
        compiler_params=_params(1),
        name="merge_out",
    )(*toks, *consts)


def _top_values(x, k):
    vals = []
    rank = jnp.full(x.shape, float(k), F32)
    for n in range(k):
        m = jnp.max(x, axis=0, keepdims=True)
        vals.append(m)
        hit = x == m
        rank = jnp.where(hit, float(n), rank)
        x = jnp.where(hit, -jnp.inf, x)
    return vals, rank


def _rows_to_block(rows):
    k, t = len(rows), rows[0].shape[1]
    idx = lax.broadcasted_iota(jnp.int32, (k, t), 0)
    out = jnp.zeros((k, t), F32)
    for n, r in enumerate(rows):
        out = jnp.where(idx == n, r, out)
    return out


def _candidate_sums(top0, top1, k):
    tile = 8
    assert k % tile == 0 and k // (tile + 1) == 1
    blk0, blk1 = _rows_to_block(top0), _rows_to_block(top1)
    row = lax.broadcasted_iota(jnp.int32, (tile, top0[0].shape[1]), 0)
    pieces = []
    for b in range(tile):
        n_a = k // (b + 1)
        for a0 in range(0, n_a, tile):
            piece = blk0[a0:a0 + tile] + top1[b]
            pieces.append(piece if a0 + tile <= n_a else jnp.where(row < n_a - a0, piece, -jnp.inf))
    for b0 in range(tile, k, tile):
        pieces.append(blk1[b0:b0 + tile] + top0[0])
    return jnp.concatenate(pieces, axis=0)


def _peer_route_kernel(cfg, hn_ref, wq_ref, keys_ref, c1_ref, r2_ref, e1_ref, e2_ref):
    nk, topk = cfg.peer_n_keys, cfg.peer_topk
    sub = cfg.peer_qdim // 2
    q = _dot(hn_ref[...], wq_ref[...]).astype(BF16)
    for h in range(cfg.peer_heads):
        sc, top, rank = [], [], []
        for half in range(2):
            col = (2 * h + half) * sub
            s = _dot_nt(keys_ref[2 * h + half], q[:, col:col + sub])
            vals, rk = _top_values(s, topk)
            sc.append(s)
            top.append(vals)
            rank.append(rk)
        best, _ = _top_values(_candidate_sums(top[0], top[1], topk), topk)
        tau = best[topk - 1]
        z = best[0] * 0.0
        for cv in best:
            z = z + jnp.exp(cv - best[0])
        count = jnp.zeros(sc[0].shape, F32)
        for b in range(topk):
            count = count + jnp.where(sc[0] + top[1][b] >= tau, 1.0, 0.0)
        rows = pl.ds(h * nk, nk)
        c1_ref[rows, :] = count
        r2_ref[rows, :] = rank[1]
        e1_ref[rows, :] = jnp.exp(sc[0] - top[0][0]) / z
        e2_ref[rows, :] = jnp.exp(sc[1] - top[1][0])


def _peer_route(cfg, hn, wq, keys, tb):
    t, d = hn.shape
    hk = cfg.peer_heads * cfg.peer_n_keys
    const = lambda a: pl.BlockSpec(a.shape, lambda i: (0,) * a.ndim, pipeline_mode=pl.Buffered(1))
    col = lambda r: pl.BlockSpec((r, tb), lambda i: (0, i))
    return pl.pallas_call(
        functools.partial(_peer_route_kernel, cfg),
        grid=(t // tb,),
        in_specs=[pl.BlockSpec((tb, d), lambda i: (i, 0)), const(wq), const(keys)],
        out_specs=[col(hk)] * 4,
        out_shape=[jax.ShapeDtypeStruct((hk, t), F32)] * 4,
        compiler_params=_params(1),
        name="peer_route",
    )(hn, wq, keys)


def _gelu(x):
    return 0.5 * x * (1.0 + lax.erf(x * (2.0 ** -0.5)))


def _peer_dense_kernel(cfg, ib, hn_ref, h_ref, u_ref, vt_ref, c1_ref, r2_ref, e1_ref, e2_ref,
                       y_ref, acc_ref, act_ref, p_ref):
    nk = cfg.peer_n_keys
    ei = pl.program_id(1)
    tb = hn_ref.shape[0]

    @pl.when(ei == 0)
    def _():
        acc_ref[...] = jnp.zeros(acc_ref.shape, F32)

    act_ref[...] = _gelu(_dot_nt(u_ref[...], hn_ref[...]))

    n_row_tiles = nk // GATE_ROWS

    def gate_tile(n, carry):
        r0 = pl.multiple_of((n % n_row_tiles) * GATE_ROWS, GATE_ROWS)
        lanes = pl.ds(pl.multiple_of((n // n_row_tiles) * LANES, LANES), LANES)
        for i0 in range(0, ib, GATE_FIRST):
            g = [None] * GATE_FIRST
            for h in range(cfg.peer_heads):
                rows = pl.ds(h * nk + r0, GATE_ROWS)
                r2, e2 = r2_ref[rows, lanes], e2_ref[rows, lanes]
                for k in range(GATE_FIRST):
                    keep = r2 < c1_ref[h, i0 + k:i0 + k + 1, lanes]
                    term = jnp.where(keep, e2, 0.0) * e1_ref[h, i0 + k:i0 + k + 1, lanes]
                    g[k] = term if h == 0 else g[k] + term
            for k in range(GATE_FIRST):
                rows = pl.ds((i0 + k) * nk + r0, GATE_ROWS)
                p_ref[rows, lanes] = (g[k] * act_ref[rows, lanes]).astype(BF16)
        return carry

    lax.fori_loop(0, n_row_tiles * (tb // LANES), gate_tile, 0)
    acc_ref[...] += _dot(vt_ref[...], p_ref[...])

    @pl.when(ei == pl.num_programs(1) - 1)
    def _():
        y_ref[...] = h_ref[...] + acc_ref[...].T


def _peer_dense(cfg, hn, h, u, vt, route, tb, ib):
    t, d = hn.shape
    nk = cfg.peer_n_keys
    eb = ib * nk
    hk = cfg.peer_heads * nk
    c1, r2, e1, e2 = route
    by_head = lambda a: a.reshape(cfg.peer_heads, nk, t)
    tokb = pl.BlockSpec((tb, d), lambda ti, ei: (ti, 0))
    col = lambda r: pl.BlockSpec((r, tb), lambda ti, ei: (0, ti))
    first = pl.BlockSpec((cfg.peer_heads, ib, tb), lambda ti, ei: (0, ei, ti))
    return pl.pallas_call(
        functools.partial(_peer_dense_kernel, cfg, ib),
        grid=(t // tb, cfg.n_experts // eb),
        in_specs=[tokb, tokb,
                  pl.BlockSpec((eb, d), lambda ti, ei: (ei, 0)),
                  pl.BlockSpec((d, eb), lambda ti, ei: (0, ei)),
                  first, col(hk), first, col(hk)],
        out_specs=tokb,
        out_shape=jax.ShapeDtypeStruct((t, d), F32),
        scratch_shapes=[pltpu.VMEM((d, tb), F32), pltpu.VMEM((eb, tb), F32), pltpu.VMEM((eb, tb), BF16)],
        compiler_params=_params(2, arbitrary_last=True),
        name="peer_dense",
    )(hn, h, u, vt, by_head(c1), r2, by_head(e1), e2)


def _prep_params(cfg, norm1_g, w_in, qk_norm_g, da_lambda, da_subln_g, gla_w_gate2, gla_b_gate, gla_norm_g,
                 w_branch_a, w_branch_b, w_out, norm2_g, peer_w_q, peer_sub_keys, peer_u, peer_v):
    d, qw, gqk, gvw, rank = cfg.d_model, cfg.da_w, cfg.gla_qk_w, cfg.gla_v_w, cfg.gla_gate_rank
    splits = (qw, qw, qw, gqk, gqk, gvw, rank, gvw, d, d)
    starts = np.concatenate([[0], np.cumsum(splits)])
    piece = lambda n: w_in[:, starts[n]:starts[n + 1]]
    glr = jnp.pad(piece(6), ((0, 0), (0, LANES - rank)))
    w_all = jnp.concatenate([piece(n) for n in (0, 1, 2, 3, 4, 5, 7, 8, 9)] + [glr], axis=1).astype(BF16)
    group = np.arange(qw) // cfg.da_head_dim
    seg = (group[:, None] == np.arange(LANES)[None, :]).astype(np.float32)
    return {
        "g1": norm1_g[None, :], "w_all": w_all,
        "qk_gain": jnp.tile(qk_norm_g, (1, qw // cfg.da_head_dim)),
        "seg": jnp.asarray(seg, BF16),
        "w_gate2": jnp.pad(gla_w_gate2, ((0, LANES - rank), (0, 0))).astype(BF16),
        "b_gate": gla_b_gate[None, :],
        "da_lambda": da_lambda, "subln_g": da_subln_g[None, :], "gla_norm_g": gla_norm_g[None, :],
        "w_a": w_branch_a.astype(BF16), "w_b": w_branch_b.astype(BF16), "w_o": w_out.astype(BF16),
        "g2": norm2_g[None, :],
        "wq": peer_w_q.astype(BF16),
        "keys": peer_sub_keys.reshape(cfg.peer_heads * 2, cfg.peer_n_keys, cfg.peer_qdim // 2).astype(BF16),
        "u": peer_u.astype(BF16), "vt": peer_v.astype(BF16).T,
    }


def _pick(n, target):
    t = min(n, target)
    while n % t:
        t //= 2
    return t


def _layer(cfg, x, cache_k, cache_v, s0, p, lam_init):
    b, l, d = x.shape
    t = b * l
    past = 0 if cache_k is None else cache_k.shape[1]
    qw = cfg.da_w

    tm = _pick(t, 256)
    if l % tm == 0:
        positions = past + jnp.arange(l)
    else:
        positions = past + (jnp.arange(tm) % l)
    cos, sin = _rope_tables(cfg, positions)
    q, k, kb, v, vb, gq, gk, gv, la, gr, ga, gb = _in_proj(cfg, x.reshape(t, d), cos, sin, p, tm)

    r3 = lambda a: a.reshape(b, l, a.shape[-1])
    if cache_k is None:
        kk, vv = r3(kb), r3(vb)
        tq = _pick(l, ATTN_TQ)
        tk = _pick(l, ATTN_TK)
    else:
        tq = _pick(l, ATTN_TQ)
        tk = -(-(past + l) // LANES) * LANES
        pad = jnp.zeros((b, tk - past - l, qw), BF16)
        kk = jnp.concatenate([cache_k.reshape(b, past, qw).astype(BF16), r3(kb), pad], axis=1)
        vv = jnp.concatenate([cache_v.reshape(b, past, qw).astype(BF16), r3(vb), pad], axis=1)
    oa = _diff_attn(cfg, r3(q), kk, vv, p["da_lambda"], p["subln_g"], past, tq, tk, lam_init)

    if s0 is None:
        s0 = jnp.zeros((b, cfg.gla_heads, cfg.gla_dk, cfg.gla_dv), F32)
    ob, s_new = _gla(cfg, r3(gq), r3(gk), r3(gv), r3(la), s0, p["gla_norm_g"], _pick(l, 512))

    h, hn = _merge_out(cfg, x.reshape(t, d), oa.reshape(t, qw), ob.reshape(t, -1), gr, ga, gb, p, tm)

    tb = _pick(t, 512)
    route = _peer_route(cfg, hn, p["wq"], p["keys"], _pick(t, ROUTE_TOKENS))
    y = _peer_dense(cfg, hn, h, p["u"], p["vt"], route, tb, PEER_FIRST_PER_STEP)

    k_new = k.reshape(b, l, cfg.da_heads, 2, cfg.da_head_dim)
    v_new = v.reshape(b, l, cfg.da_heads, cfg.da_v_dim)
    return y.reshape(b, l, d), k_new, v_new, s_new


def _forward(cfg, x_prompt, x_sample, cache_k, cache_v, state_gla, *weights):
    yp, ys = x_prompt, x_sample
    outs = [[] for _ in range(6)]
    for layer in range(cache_k.shape[0]):
        p = _prep_params(cfg, *[w[layer] for w in weights])
        lam_init = 0.8 - 0.6 * math.exp(-0.3 * layer)
        yp, kp, vp, sp = _layer(cfg, yp, None, None, None, p, lam_init)
        ys, kn, vn, sn = _layer(cfg, ys, cache_k[layer], cache_v[layer], state_gla[layer], p, lam_init)
        for lst, a in zip(outs, (kp, vp, sp, kn, vn, sn)):
            lst.append(a)
    return (yp, ys) + tuple(jnp.stack(o) for o in outs)


def kernel(x_prompt, x_sample, cache_k, cache_v, state_gla, norm1_g, w_in, qk_norm_g, da_lambda, da_subln_g,
           gla_w_gate2, gla_b_gate, gla_norm_g, w_branch_a, w_branch_b, w_out, norm2_g, peer_w_q,
           peer_sub_keys, peer_u, peer_v):
    return _forward(Cfg(), x_prompt, x_sample, cache_k, cache_v, state_gla, norm1_g, w_in, qk_norm_g, da_lambda,
                    da_subln_g, gla_w_gate2, gla_b_gate, gla_norm_g, w_branch_a, w_branch_b, w_out, norm2_g,
                    peer_w_q, peer_sub_keys, peer_u, peer_v)
```

```python
import functools
import math
from typing import NamedTuple

import jax
import jax.numpy as jnp
import numpy as np
from jax import lax
from jax.experimental import pallas as pl
from jax.experimental.pallas import tpu as pltpu

F32 = jnp.float32
BF16 = jnp.bfloat16
LANES = 128
VMEM_LIMIT = 56 * 1024 * 1024
GATE_ROWS = 16
ROUTE_TOKENS = 256
PEER_FIRST_PER_STEP = 16
GATE_FIRST = 8
ATTN_TQ = 512
ATTN_TK = 512
ATTN_MAX_STATIC_Q_BLOCKS = 1


class Cfg(NamedTuple):
    d_model: int = 1024
    chunk: int = 64
    eps: float = 1e-6
    da_heads: int = 8
    da_head_dim: int = 64
    rope_theta: float = 10000.0
    gla_heads: int = 4
    gla_gate_rank: int = 16
    gla_gate_tau: float = 16.0
    peer_heads: int = 8
    peer_n_keys: int = 128
    peer_qdim: int = 256
    peer_topk: int = 16

    @property
    def da_v_dim(self): return 2 * self.da_head_dim
    @property
    def da_w(self): return self.da_heads * 2 * self.da_head_dim
    @property
    def gla_dk(self): return self.d_model // 2 // self.gla_heads
    @property
    def gla_dv(self): return self.d_model // self.gla_heads
    @property
    def gla_qk_w(self): return self.gla_heads * self.gla_dk
    @property
    def gla_v_w(self): return self.gla_heads * self.gla_dv
    @property
    def n_experts(self): return self.peer_n_keys * self.peer_n_keys


def _dot(a, b):
    return jnp.dot(a, b, preferred_element_type=F32)


def _dot_nt(a, b):
    return lax.dot_general(a, b, (((1,), (1,)), ((), ())), preferred_element_type=F32)


def _split_bf16(t):
    hi = t.astype(BF16)
    lo = (t - hi.astype(F32)).astype(BF16)
    return hi, lo


def _params(n_axes, arbitrary_last=False):
    sem = ["parallel"] * n_axes
    if arbitrary_last:
        sem[-1] = "arbitrary"
    return pltpu.CompilerParams(dimension_semantics=tuple(sem), vmem_limit_bytes=VMEM_LIMIT)


def _rope_table_kernel(pos_ref, freq_ref, sign_ref, cos_ref, sin_ref):
    ang = pos_ref[...] * freq_ref[...]
    cos_ref[...] = jnp.cos(ang)
    sin_ref[...] = jnp.sin(ang) * sign_ref[...]


def _rope_tables(cfg, positions):
    half = cfg.da_head_dim // 2
    r = positions.shape[0]
    lane = np.arange(LANES)
    inv_freq = cfg.rope_theta ** (-jnp.arange(half, dtype=F32) / half)
    freq = jnp.tile(inv_freq, LANES // half)[None, :]
    sign = jnp.asarray(np.where(lane % cfg.da_head_dim < half, -1.0, 1.0), F32)[None, :]
    pos = jnp.broadcast_to(positions.astype(F32)[:, None], (r, LANES))
    full = lambda shape: pl.BlockSpec(shape, lambda: (0, 0))
    return pl.pallas_call(
        _rope_table_kernel,
        out_shape=(jax.ShapeDtypeStruct((r, LANES), F32),) * 2,
        in_specs=[full((r, LANES)), full((1, LANES)), full((1, LANES))],
        out_specs=(full((r, LANES)),) * 2,
        name="rope_table",
    )(pos, freq, sign)


def _log_sigmoid(x):
    return jnp.minimum(x, 0.0) - jnp.log1p(jnp.exp(-jnp.abs(x)))


def _in_proj_kernel(cfg, x_ref, g1_ref, w_ref, cos_ref, sin_ref, qkg_ref, seg_ref, w2_ref, b2_ref,
                    q_out, k_out, kb_out, v_out, vb_out, gq_out, gk_out, gv_out, la_out, gr_out, ga_out, gb_out):
    d = cfg.d_model
    qw, gqk, gvw = cfg.da_w, cfg.gla_qk_w, cfg.gla_v_w
    x = x_ref[...]
    xn = (x * lax.rsqrt(jnp.mean(x * x, axis=-1, keepdims=True) + cfg.eps) * g1_ref[...]).astype(BF16)

    widths = (("q", qw), ("k", qw), ("v", qw), ("gq", gqk), ("gk", gqk), ("gv", gvw), ("gr", gvw),
              ("ga", d), ("gb", d), ("glr", LANES))
    cols, start = {}, 0
    for name, width in widths:
        cols[name] = slice(start, start + width)
        start += width

    def proj(name):
        return _dot(xn, w_ref[:, cols[name]])

    nrep = qw // LANES
    cos = jnp.tile(cos_ref[...], (1, nrep))
    sin = jnp.tile(sin_ref[...], (1, nrep))
    lane = lax.broadcasted_iota(jnp.int32, (x.shape[0], qw), 1)
    first_half = (lane & (cfg.da_head_dim - 1)) < (cfg.da_head_dim // 2)
    seg = seg_ref[...]
    lane128 = lax.broadcasted_iota(jnp.int32, (x.shape[0], LANES), 1)

    def group_sums(t):
        hi, lo = _split_bf16(t * t)
        return _dot(hi, seg) + _dot(lo, seg)

    def qk_norm_rope(t, ss, gain):
        r = lax.rsqrt(ss * (1.0 / cfg.da_head_dim) + cfg.eps)
        per_tile = LANES // cfg.da_head_dim
        spread = []
        for v in range(nrep):
            tile = r[:, v * per_tile:v * per_tile + 1]
            for g in range(1, per_tile):
                col = v * per_tile + g
                tile = jnp.where(lane128 < g * cfg.da_head_dim, tile, r[:, col:col + 1])
            spread.append(jnp.broadcast_to(tile, (t.shape[0], LANES)))
        y = t * jnp.concatenate(spread, axis=1) * gain
        half = cfg.da_head_dim // 2
        partner = jnp.where(first_half, pltpu.roll(y, qw - half, 1), pltpu.roll(y, half, 1))
        return y * cos + partner * sin

    q_raw = proj("q")
    q = qk_norm_rope(q_raw, group_sums(q_raw), qkg_ref[0:1, :])
    q_out[...] = (q * (cfg.da_head_dim ** -0.5 * math.log2(math.e))).astype(BF16)
    k_raw = proj("k")
    k = qk_norm_rope(k_raw, group_sums(k_raw), qkg_ref[1:2, :])
    for hm in range(qw // cfg.da_head_dim):
        k_out[:, hm // 2, hm % 2, :] = k[:, hm * cfg.da_head_dim:(hm + 1) * cfg.da_head_dim]
    kb_out[...] = k.astype(BF16)
    v = proj("v")
    v_out[...] = v
    vb_out[...] = v.astype(BF16)
    gq_out[...] = proj("gq") * (cfg.gla_dk ** -0.5)
    gk_out[...] = proj("gk")
    gv_out[...] = proj("gv")
    gr = proj("gr")
    gr_out[...] = gr * jax.nn.sigmoid(gr)
    ga_out[...] = jax.nn.sigmoid(proj("ga"))
    gb_out[...] = jax.nn.sigmoid(proj("gb"))
    gate = _dot(proj("glr").astype(BF16), w2_ref[...]) + b2_ref[...]
    la_out[...] = _log_sigmoid(gate) * (1.0 / cfg.gla_gate_tau)


def _in_proj(cfg, x, cos, sin, p, tm):
    t, d = x.shape
    qw, gqk, gvw = cfg.da_w, cfg.gla_qk_w, cfg.gla_v_w
    n_tab = cos.shape[0] // tm
    tok = lambda w: pl.BlockSpec((tm, w), lambda i: (i, 0))
    const = lambda a: pl.BlockSpec(a.shape, lambda i: (0,) * a.ndim, pipeline_mode=pl.Buffered(1))
    tab = pl.BlockSpec((tm, LANES), lambda i: (i % n_tab, 0))
    consts = [p["g1"], p["w_all"]]
    consts2 = [p["qk_gain"], p["seg"], p["w_gate2"], p["b_gate"]]
    outs = [(qw, BF16), (qw, F32), (qw, BF16), (qw, F32), (qw, BF16), (gqk, F32), (gqk, F32), (gvw, F32),
            (gqk, F32), (gvw, F32), (d, F32), (d, F32)]
    cache_shape = {1: (t, cfg.da_heads, 2, cfg.da_head_dim)}
    cache_spec = {n: pl.BlockSpec((tm,) + s[1:], lambda i, nd=len(s): (i,) + (0,) * (nd - 1))
                  for n, s in cache_shape.items()}
    return pl.pallas_call(
        functools.partial(_in_proj_kernel, cfg),
        grid=(t // tm,),
        in_specs=[tok(d)] + [const(a) for a in consts] + [tab, tab] + [const(a) for a in consts2],
        out_specs=[cache_spec.get(n, tok(w)) for n, (w, _) in enumerate(outs)],
        out_shape=[jax.ShapeDtypeStruct(cache_shape.get(n, (t, w)), dt) for n, (w, dt) in enumerate(outs)],
        compiler_params=_params(1),
        name="in_proj",
    )(x, *consts, cos, sin, *consts2)


def _diff_attn_kernel(cfg, past, tq, tk, n_q, lam_init, q_ref, *refs):
    qi = pl.program_id(2)
    hd = cfg.da_head_dim
    q = q_ref[0]
    lane = lax.broadcasted_iota(jnp.int32, q.shape, 1)
    zero = jnp.zeros_like(q)
    qs = jnp.concatenate([jnp.where(lane < hd, q, zero), jnp.where(lane >= hd, q, zero)], axis=0)
    attend = functools.partial(_attend, cfg, tq, tk, lam_init, refs, qs)

    def blocks(q_start):
        n_blocks = (q_start + tq + tk - 1) // tk
        n_full = (q_start + cfg.chunk) // tk
        return n_full, n_blocks

    if n_q <= ATTN_MAX_STATIC_Q_BLOCKS:
        for c in range(n_q):
            pl.when(qi == c)(functools.partial(attend, past + c * tq, *blocks(past + c * tq), True))
    else:
        q_start = past + qi * tq
        attend(q_start, *blocks(q_start), False)


def _attend(cfg, tq, tk, lam_init, refs, qs, q_start, n_full, n_blocks, static):
    k_ref, v_ref, lamp_ref, g_ref, o_ref, s_ref, mpart_ref, mrow_ref, lpart_ref, acc_ref = refs
    shift = int(math.log2(cfg.chunk))
    n_lane_chunks = tk // LANES
    mpart_ref[...] = jnp.full(mpart_ref.shape, -jnp.inf, F32)

    def scores(j, masked):
        ks = j * tk if static else pl.multiple_of(j * tk, tk)
        s = _dot_nt(qs, k_ref[0, pl.ds(ks, tk), :])
        if masked:
            row = lax.broadcasted_iota(jnp.int32, s.shape, 0)
            col = lax.broadcasted_iota(jnp.int32, s.shape, 1)
            qpos = q_start + jnp.where(row >= tq, row - tq, row)
            allowed = lax.shift_right_logical(ks + col, shift) <= lax.shift_right_logical(qpos, shift)
            s = jnp.where(allowed, s, -jnp.inf)
        s_ref[j] = s
        part = s[:, 0:LANES]
        for c in range(1, n_lane_chunks):
            part = jnp.maximum(part, s[:, c * LANES:(c + 1) * LANES])
        mpart_ref[...] = jnp.maximum(mpart_ref[...], part)

    def in_pairs(fn, lo, hi):
        def pair(n, carry):
            fn(lo + 2 * n)
            fn(lo + 2 * n + 1)
            return carry

        lax.fori_loop(0, (hi - lo) // 2, pair, 0)

        @pl.when((hi - lo) % 2 == 1)
        def _():
            fn(hi - 1)

    if static:
        for j in range(n_blocks):
            scores(j, j >= n_full)
    else:
        in_pairs(lambda j: scores(j, False), 0, n_full)
        in_pairs(lambda j: scores(j, True), n_full, n_blocks)

    mrow_ref[...] = jnp.broadcast_to(jnp.max(mpart_ref[...], axis=-1, keepdims=True), mrow_ref.shape)
    lpart_ref[...] = jnp.zeros(lpart_ref.shape, F32)
    acc_ref[...] = jnp.zeros(acc_ref.shape, F32)

    def weigh(j, carry):
        ks = j * tk if static else pl.multiple_of(j * tk, tk)
        m = mrow_ref[...]
        ps = [jnp.exp2(s_ref[j, :, c * LANES:(c + 1) * LANES] - m) for c in range(n_lane_chunks)]
        lsum = ps[0]
        for pc in ps[1:]:
            lsum = lsum + pc
        lpart_ref[...] += lsum
        p = jnp.concatenate([pc.astype(BF16) for pc in ps], axis=1)
        acc_ref[...] += _dot(p, v_ref[0, pl.ds(ks, tk), :])
        return carry

    if static:
        for j in range(n_blocks):
            weigh(j, 0)
    else:
        in_pairs(lambda j: weigh(j, 0), 0, n_blocks)

    lp = lamp_ref[...]
    lam = (jnp.exp(jnp.sum(lp[0:1] * lp[1:2], axis=-1, keepdims=True))
           - jnp.exp(jnp.sum(lp[2:3] * lp[3:4], axis=-1, keepdims=True)) + lam_init)
    o = acc_ref[...] / jnp.sum(lpart_ref[...], axis=-1, keepdims=True)
    o = o[:tq] - lam * o[tq:]
    o = o * lax.rsqrt(jnp.mean(o * o, axis=-1, keepdims=True) + cfg.eps) * g_ref[...] * (1.0 - lam_init)
    o_ref[0] = o.astype(BF16)


def _diff_attn(cfg, q, k, v, lam_p, subln_g, past, tq, tk, lam_init):
    b, lq, w = q.shape
    lk = k.shape[1]
    vd = cfg.da_v_dim
    assert lq % tq == 0 and lk % tk == 0 and tk % LANES == 0 and tq % cfg.chunk == 0 and past % cfg.chunk == 0
    assert past + lq <= lk
    kv_spec = pl.BlockSpec((1, lk, vd), lambda bi, h, qi: (bi, 0, h))
    q_spec = pl.BlockSpec((1, tq, vd), lambda bi, h, qi: (bi, qi, h))
    return pl.pallas_call(
        functools.partial(_diff_attn_kernel, cfg, past, tq, tk, lq // tq, lam_init),
        grid=(b, cfg.da_heads, lq // tq),
        in_specs=[q_spec, kv_spec, kv_spec,
                  pl.BlockSpec(lam_p.shape, lambda bi, h, qi: (0, 0)),
                  pl.BlockSpec(subln_g.shape, lambda bi, h, qi: (0, 0))],
        out_specs=q_spec,
        out_shape=jax.ShapeDtypeStruct((b, lq, w), BF16),
        scratch_shapes=[pltpu.VMEM((lk // tk, 2 * tq, tk), F32), pltpu.VMEM((2 * tq, LANES), F32),
                        pltpu.VMEM((2 * tq, LANES), F32), pltpu.VMEM((2 * tq, LANES), F32),
                        pltpu.VMEM((2 * tq, vd), F32)],
        compiler_params=_params(3),
        name="diff_attn",
    )(q, k, v, lam_p, subln_g)


def _gla_constants(c):
    r = np.arange(c)
    i = r[:, None]
    rr = r[None, :]
    mats = [(rr <= i), (rr > i)]
    levels = []
    s = c // 2
    while s >= 1:
        levels.append(s)
        s //= 2
    for s in levels:
        bd = (i // s) * s
        mats.append((rr > bd) & (rr <= i))
        bd2 = (i // s + 1) * s
        mats.append((rr > i) & (rr <= bd2))
    stack = np.concatenate([m.astype(np.float32) for m in mats], axis=0)
    lev = np.full((c, c), -1, np.int32)
    jj = r[None, :]
    for n, s in enumerate(levels):
        sel = (jj < i) & (i // s != jj // s) & (lev < 0)
        lev[sel] = n
    lev[i == jj] = len(levels)
    return stack, lev, len(levels)


def _gla_kernel(cfg, n_levels, n_sub, q_ref, k_ref, v_ref, a_ref, s0_ref, mat_ref, lev_ref, g_ref,
                o_ref, s_out_ref, st_ref):
    c, dk, dv = cfg.chunk, cfg.gla_dk, cfg.gla_dv
    step_i = pl.program_id(1)

    @pl.when(step_i == 0)
    def _():
        for hd in range(cfg.gla_heads):
            st_ref[hd] = s0_ref[0, hd].T

    mat = mat_ref[...]
    lev = lev_ref[...]

    heads = range(cfg.gla_heads)
    kcols = [slice(hd * dk, (hd + 1) * dk) for hd in heads]
    vcols = [slice(hd * dv, (hd + 1) * dv) for hd in heads]

    def chunk(n, carry):
        rows = pl.ds(pl.multiple_of(n * c, c), c)
        a_hi, a_lo = _split_bf16(a_ref[0, rows, :])
        e_all = _dot(mat, a_hi) + _dot(mat, a_lo)
        e = [e_all[:, kcols[hd]] for hd in heads]
        q = [q_ref[0, rows, kcols[hd]] for hd in heads]
        k = [k_ref[0, rows, kcols[hd]] for hd in heads]
        vb = [v_ref[0, rows, vcols[hd]].astype(BF16) for hd in heads]

        att = [jnp.where(lev == n_levels, _dot_nt(q[hd].astype(BF16), k[hd].astype(BF16)), 0.0) for hd in heads]
        for n_l in range(n_levels):
            eq = slice((2 + 2 * n_l) * c, (3 + 2 * n_l) * c)
            ek = slice((3 + 2 * n_l) * c, (4 + 2 * n_l) * c)
            a_l = [_dot_nt((q[hd] * jnp.exp(e[hd][eq])).astype(BF16), (k[hd] * jnp.exp(e[hd][ek])).astype(BF16))
                   for hd in heads]
            att = [jnp.where(lev == n_l, a_l[hd], att[hd]) for hd in heads]

        st = [st_ref[hd] for hd in heads]
        o = [_dot_nt((q[hd] * jnp.exp(e[hd][0:c])).astype(BF16), st[hd].astype(BF16))
             + _dot(att[hd].astype(BF16), vb[hd]) for hd in heads]
        kl = [(k[hd] * jnp.exp(e[hd][c:2 * c])).astype(BF16) for hd in heads]
        upd = [lax.dot_general(vb[hd], kl[hd], (((0,), (0,)), ((), ())), preferred_element_type=F32)
               for hd in heads]
        for hd in heads:
            st_ref[hd] = st[hd] * jnp.exp(e[hd][c - 1:c, :]) + upd[hd]
            on = o[hd] * lax.rsqrt(jnp.mean(o[hd] * o[hd], axis=-1, keepdims=True) + cfg.eps) * g_ref[...]
            o_ref[0, rows, vcols[hd]] = on
        return carry

    lax.fori_loop(0, n_sub, chunk, 0, unroll=2 if n_sub % 2 == 0 else 1)

    @pl.when(step_i == pl.num_programs(1) - 1)
    def _():
        for hd in range(cfg.gla_heads):
            s_out_ref[0, hd] = st_ref[hd].T


def _gla(cfg, gq, gk, gv, la, s0, norm_g, lb):
    b, l, _ = gq.shape
    c, dk, dv, h = cfg.chunk, cfg.gla_dk, cfg.gla_dv, cfg.gla_heads
    assert l % lb == 0 and lb % c == 0
    stack, lev, n_levels = _gla_constants(c)
    mat = jnp.asarray(stack, BF16)
    lev = jnp.asarray(lev)
    seq = lambda w: pl.BlockSpec((1, lb, h * w), lambda bi, si: (bi, si, 0))
    state = pl.BlockSpec((1, h, dk, dv), lambda bi, si: (bi, 0, 0, 0))
    const = lambda a: pl.BlockSpec(a.shape, lambda bi, si: (0,) * a.ndim)
    return pl.pallas_call(
        functools.partial(_gla_kernel, cfg, n_levels, lb // c),
        grid=(b, l // lb),
        in_specs=[seq(dk), seq(dk), seq(dv), seq(dk), state, const(mat), const(lev), const(norm_g)],
        out_specs=[seq(dv), state],
        out_shape=[jax.ShapeDtypeStruct((b, l, h * dv), F32), jax.ShapeDtypeStruct((b, h, dk, dv), F32)],
        scratch_shapes=[pltpu.VMEM((h, dv, dk), F32)],
        compiler_params=_params(2, arbitrary_last=True),
        name="gla",
    )(gq, gk, gv, la, s0, mat, lev, norm_g)


def _merge_kernel(cfg, x_ref, oa_ref, ob_ref, gr_ref, ga_ref, gb_ref, wa_ref, wb_ref, wo_ref, g2_ref,
                  h_ref, hn_ref):
    a = _dot(oa_ref[...], wa_ref[...])
    bb = _dot((ob_ref[...] * gr_ref[...]).astype(BF16), wb_ref[...])
    mix = ga_ref[...] * a + gb_ref[...] * bb
    h = x_ref[...] + _dot(mix.astype(BF16), wo_ref[...])
    h_ref[...] = h
    hn = h * lax.rsqrt(jnp.mean(h * h, axis=-1, keepdims=True) + cfg.eps) * g2_ref[...]
    hn_ref[...] = hn.astype(BF16)


def _merge_out(cfg, x, oa, ob, gr, ga, gb, p, tm):
    t, d = x.shape
    tok = lambda a: pl.BlockSpec((tm, a.shape[1]), lambda i: (i, 0))
    const = lambda a: pl.BlockSpec(a.shape, lambda i: (0,) * a.ndim, pipeline_mode=pl.Buffered(1))
    toks = [x, oa, ob, gr, ga, gb]
    consts = [p["w_a"], p["w_b"], p["w_o"], p["g2"]]
    return pl.pallas_call(
        functools.partial(_merge_kernel, cfg),
        grid=(t // tm,),
        in_specs=[tok(a) for a in toks] + [const(a) for a in consts],
        out_specs=[pl.BlockSpec((tm, d), lambda i: (i, 0))] * 2,
        out_shape=[jax.ShapeDtypeStruct((t, d), F32), jax.ShapeDtypeStruct((t, d), BF16)],
        compiler_params=_params(1),
        name="merge_out",
    )(*toks, *consts)


def _top_values(x, k, with_rank=False):
    vals = []
    rank = jnp.full(x.shape, float(k), F32) if with_rank else None
    for n in range(k):
        m = jnp.max(x, axis=0, keepdims=True)
        vals.append(m)
        hit = x == m
        if with_rank:
            rank = jnp.where(hit, float(n), rank)
        x = jnp.where(hit, -jnp.inf, x)
    return (vals, rank) if with_rank else vals


def _rows_to_block(rows):
    k, t = len(rows), rows[0].shape[1]
    idx = lax.broadcasted_iota(jnp.int32, (k, t), 0)
    out = jnp.zeros((k, t), F32)
    for n, r in enumerate(rows):
        out = jnp.where(idx == n, r, out)
    return out


def _candidate_sums(top0, top1, k):
    tile = 8
    assert k % tile == 0 and k // (tile + 1) == 1
    blk0, blk1 = _rows_to_block(top0), _rows_to_block(top1)
    row = lax.broadcasted_iota(jnp.int32, (tile, top0[0].shape[1]), 0)
    pieces = []
    for b in range(tile):
        n_a = k // (b + 1)
        for a0 in range(0, n_a, tile):
            piece = blk0[a0:a0 + tile] + top1[b]
            pieces.append(piece if a0 + tile <= n_a else jnp.where(row < n_a - a0, piece, -jnp.inf))
    for b0 in range(tile, k, tile):
        pieces.append(blk1[b0:b0 + tile] + top0[0])
    return jnp.concatenate(pieces, axis=0)


def _peer_route_kernel(cfg, hn_ref, wq_ref, keys_ref, c1_ref, r2_ref, e1_ref, e2_ref):
    nk, topk = cfg.peer_n_keys, cfg.peer_topk
    sub = cfg.peer_qdim // 2
    q = _dot(hn_ref[...], wq_ref[...]).astype(BF16)
    for h in range(cfg.peer_heads):
        sc = [_dot_nt(keys_ref[2 * h + half], q[:, (2 * h + half) * sub:(2 * h + half + 1) * sub])
              for half in range(2)]
        top0 = _top_values(sc[0], topk)
        top1, rank1 = _top_values(sc[1], topk, with_rank=True)
        top = [top0, top1]
        best = _top_values(_candidate_sums(top0, top1, topk), topk)
        tau = best[topk - 1]
        z = best[0] * 0.0
        for cv in best:
            z = z + jnp.exp(cv - best[0])
        count = jnp.zeros(sc[0].shape, F32)
        for b in range(topk):
            count = count + jnp.where(sc[0] + top[1][b] >= tau, 1.0, 0.0)
        rows = pl.ds(h * nk, nk)
        c1_ref[rows, :] = count
        r2_ref[rows, :] = rank1
        e1_ref[rows, :] = jnp.exp(sc[0] - top[0][0]) / z
        e2_ref[rows, :] = jnp.exp(sc[1] - top[1][0])


def _peer_route(cfg, hn, wq, keys, tb):
    t, d = hn.shape
    hk = cfg.peer_heads * cfg.peer_n_keys
    const = lambda a: pl.BlockSpec(a.shape, lambda i: (0,) * a.ndim, pipeline_mode=pl.Buffered(1))
    col = lambda r: pl.BlockSpec((r, tb), lambda i: (0, i))
    return pl.pallas_call(
        functools.partial(_peer_route_kernel, cfg),
        grid=(t // tb,),
        in_specs=[pl.BlockSpec((tb, d), lambda i: (i, 0)), const(wq), const(keys)],
        out_specs=[col(hk)] * 4,
        out_shape=[jax.ShapeDtypeStruct((hk, t), F32)] * 4,
        compiler_params=_params(1),
        name="peer_route",
    )(hn, wq, keys)


def _gelu(x):
    return 0.5 * x * (1.0 + lax.erf(x * (2.0 ** -0.5)))


def _peer_dense_kernel(cfg, ib, hn_ref, h_ref, u_ref, vt_ref, c1_ref, r2_ref, e1_ref, e2_ref,
                       y_ref, acc_ref, act_ref, p_ref):
    nk = cfg.peer_n_keys
    ei = pl.program_id(1)
    tb = hn_ref.shape[0]

    @pl.when(ei == 0)
    def _():
        acc_ref[...] = jnp.zeros(acc_ref.shape, F32)

    act_ref[...] = _gelu(_dot_nt(u_ref[...], hn_ref[...]))

    n_row_tiles = nk // GATE_ROWS

    def gate_tile(n, carry):
        r0 = pl.multiple_of((n % n_row_tiles) * GATE_ROWS, GATE_ROWS)
        lanes = pl.ds(pl.multiple_of((n // n_row_tiles) * LANES, LANES), LANES)
        for i0 in range(0, ib, GATE_FIRST):
            g = [None] * GATE_FIRST
            for h in range(cfg.peer_heads):
                rows = pl.ds(h * nk + r0, GATE_ROWS)
                r2, e2 = r2_ref[rows, lanes], e2_ref[rows, lanes]
                for k in range(GATE_FIRST):
                    keep = r2 < c1_ref[h, i0 + k:i0 + k + 1, lanes]
                    term = jnp.where(keep, e2, 0.0) * e1_ref[h, i0 + k:i0 + k + 1, lanes]
                    g[k] = term if h == 0 else g[k] + term
            for k in range(GATE_FIRST):
                rows = pl.ds((i0 + k) * nk + r0, GATE_ROWS)
                p_ref[rows, lanes] = (g[k] * act_ref[rows, lanes]).astype(BF16)
        return carry

    lax.fori_loop(0, n_row_tiles * (tb // LANES), gate_tile, 0)
    acc_ref[...] += _dot(vt_ref[...], p_ref[...])

    @pl.when(ei == pl.num_programs(1) - 1)
    def _():
        y_ref[...] = h_ref[...] + acc_ref[...].T


def _peer_dense(cfg, hn, h, u, vt, route, tb, ib):
    t, d = hn.shape
    nk = cfg.peer_n_keys
    eb = ib * nk
    hk = cfg.peer_heads * nk
    c1, r2, e1, e2 = route
    by_head = lambda a: a.reshape(cfg.peer_heads, nk, t)
    tokb = pl.BlockSpec((tb, d), lambda ti, ei: (ti, 0))
    col = lambda r: pl.BlockSpec((r, tb), lambda ti, ei: (0, ti))
    first = pl.BlockSpec((cfg.peer_heads, ib, tb), lambda ti, ei: (0, ei, ti))
    return pl.pallas_call(
        functools.partial(_peer_dense_kernel, cfg, ib),
        grid=(t // tb, cfg.n_experts // eb),
        in_specs=[tokb, tokb,
                  pl.BlockSpec((eb, d), lambda ti, ei: (ei, 0)),
                  pl.BlockSpec((d, eb), lambda ti, ei: (0, ei)),
                  first, col(hk), first, col(hk)],
        out_specs=tokb,
        out_shape=jax.ShapeDtypeStruct((t, d), F32),
        scratch_shapes=[pltpu.VMEM((d, tb), F32), pltpu.VMEM((eb, tb), F32), pltpu.VMEM((eb, tb), BF16)],
        compiler_params=_params(2, arbitrary_last=True),
        name="peer_dense",
    )(hn, h, u, vt, by_head(c1), r2, by_head(e1), e2)


def _prep_params(cfg, norm1_g, w_in, qk_norm_g, da_lambda, da_subln_g, gla_w_gate2, gla_b_gate, gla_norm_g,
                 w_branch_a, w_branch_b, w_out, norm2_g, peer_w_q, peer_sub_keys, peer_u, peer_v):
    d, qw, gqk, gvw, rank = cfg.d_model, cfg.da_w, cfg.gla_qk_w, cfg.gla_v_w, cfg.gla_gate_rank
    splits = (qw, qw, qw, gqk, gqk, gvw, rank, gvw, d, d)
    starts = np.concatenate([[0], np.cumsum(splits)])
    piece = lambda n: w_in[:, starts[n]:starts[n + 1]]
    glr = jnp.pad(piece(6), ((0, 0), (0, LANES - rank)))
    w_all = jnp.concatenate([piece(n) for n in (0, 1, 2, 3, 4, 5, 7, 8, 9)] + [glr], axis=1).astype(BF16)
    group = np.arange(qw) // cfg.da_head_dim
    seg = (group[:, None] == np.arange(LANES)[None, :]).astype(np.float32)
    return {
        "g1": norm1_g[None, :], "w_all": w_all,
        "qk_gain": jnp.tile(qk_norm_g, (1, qw // cfg.da_head_dim)),
        "seg": jnp.asarray(seg, BF16),
        "w_gate2": jnp.pad(gla_w_gate2, ((0, LANES - rank), (0, 0))).astype(BF16),
        "b_gate": gla_b_gate[None, :],
        "da_lambda": da_lambda, "subln_g": da_subln_g[None, :], "gla_norm_g": gla_norm_g[None, :],
        "w_a": w_branch_a.astype(BF16), "w_b": w_branch_b.astype(BF16), "w_o": w_out.astype(BF16),
        "g2": norm2_g[None, :],
        "wq": peer_w_q.astype(BF16),
        "keys": peer_sub_keys.reshape(cfg.peer_heads * 2, cfg.peer_n_keys, cfg.peer_qdim // 2).astype(BF16),
        "u": peer_u.astype(BF16), "vt": peer_v.astype(BF16).T,
    }


def _pick(n, target):
    t = min(n, target)
    while n % t:
        t //= 2
    return t


def _layer(cfg, x, cache_k, cache_v, s0, p, lam_init):
    b, l, d = x.shape
    t = b * l
    past = 0 if cache_k is None else cache_k.shape[1]
    qw = cfg.da_w

    tm = _pick(t, 256)
    if l % tm == 0:
        positions = past + jnp.arange(l)
    else:
        positions = past + (jnp.arange(tm) % l)
    cos, sin = _rope_tables(cfg, positions)
    q, k, kb, v, vb, gq, gk, gv, la, gr, ga, gb = _in_proj(cfg, x.reshape(t, d), cos, sin, p, tm)

    r3 = lambda a: a.reshape(b, l, a.shape[-1])
    if cache_k is None:
        kk, vv = r3(kb), r3(vb)
        tq = _pick(l, ATTN_TQ)
        tk = _pick(l, ATTN_TK)
    else:
        tq = _pick(l, ATTN_TQ)
        tk = -(-(past + l) // LANES) * LANES
        pad = jnp.zeros((b, tk - past - l, qw), BF16)
        kk = jnp.concatenate([cache_k.reshape(b, past, qw).astype(BF16), r3(kb), pad], axis=1)
        vv = jnp.concatenate([cache_v.reshape(b, past, qw).astype(BF16), r3(vb), pad], axis=1)
    oa = _diff_attn(cfg, r3(q), kk, vv, p["da_lambda"], p["subln_g"], past, tq, tk, lam_init)

    if s0 is None:
        s0 = jnp.zeros((b, cfg.gla_heads, cfg.gla_dk, cfg.gla_dv), F32)
    ob, s_new = _gla(cfg, r3(gq), r3(gk), r3(gv), r3(la), s0, p["gla_norm_g"], _pick(l, 512))

    h, hn = _merge_out(cfg, x.reshape(t, d), oa.reshape(t, qw), ob.reshape(t, -1), gr, ga, gb, p, tm)

    tb = _pick(t, 512)
    route = _peer_route(cfg, hn, p["wq"], p["keys"], _pick(t, ROUTE_TOKENS))
    y = _peer_dense(cfg, hn, h, p["u"], p["vt"], route, tb, PEER_FIRST_PER_STEP)

    k_new = k.reshape(b, l, cfg.da_heads, 2, cfg.da_head_dim)
    v_new = v.reshape(b, l, cfg.da_heads, cfg.da_v_dim)
    return y.reshape(b, l, d), k_new, v_new, s_new


def _forward(cfg, x_prompt, x_sample, cache_k, cache_v, state_gla, *weights):
    yp, ys = x_prompt, x_sample
    outs = [[] for _ in range(6)]
    for layer in range(cache_k.shape[0]):
        p = _prep_params(cfg, *[w[layer] for w in weights])
        lam_init = 0.8 - 0.6 * math.exp(-0.3 * layer)
        yp, kp, vp, sp = _layer(cfg, yp, None, None, None, p, lam_init)
        ys, kn, vn, sn = _layer(cfg, ys, cache_k[layer], cache_v[layer], state_gla[layer], p, lam_init)
        for lst, a in zip(outs, (kp, vp, sp, kn, vn, sn)):
            lst.append(a)
    return (yp, ys) + tuple(jnp.stack(o) for o in outs)


def kernel(x_prompt, x_sample, cache_k, cache_v, state_gla, norm1_g, w_in, qk_norm_g, da_lambda, da_subln_g,
           gla_w_gate2, gla_b_gate, gla_norm_g, w_branch_a, w_branch_b, w_out, norm2_g, peer_w_q,
           peer_sub_keys, peer_u, peer_v):
    return _forward(Cfg(), x_prompt, x_sample, cache_k, cache_v, state_gla, norm1_g, w_in, qk_norm_g, da_lambda,
                    da_subln_g, gla_w_gate2, gla_b_gate, gla_norm_g, w_branch_a, w_branch_b, w_out, norm2_g,
                    peer_w_q, peer_sub_keys, peer_u, peer_v)
```

```python
import functools
import math
from typing import NamedTuple

import jax
import jax.numpy as jnp
import numpy as np
from jax import lax
from jax.experimental import pallas as pl
from jax.experimental.pallas import tpu as pltpu

F32 = jnp.float32
BF16 = jnp.bfloat16
LANES = 128
VMEM_LIMIT = 56 * 1024 * 1024
GATE_ROWS = 16
ROUTE_TOKENS = 256
PEER_FIRST_PER_STEP = 16
GATE_FIRST = 8
ATTN_TQ = 512
ATTN_TK = 512
ATTN_MAX_STATIC_Q_BLOCKS = 1


class Cfg(NamedTuple):
    d_model: int = 1024
    chunk: int = 64
    eps: float = 1e-6
    da_heads: int = 8
    da_head_dim: int = 64
    rope_theta: float = 10000.0
    gla_heads: int = 4
    gla_gate_rank: int = 16
    gla_gate_tau: float = 16.0
    peer_heads: int = 8
    peer_n_keys: int = 128
    peer_qdim: int = 256
    peer_topk: int = 16

    @property
    def da_v_dim(self): return 2 * self.da_head_dim
    @property
    def da_w(self): return self.da_heads * 2 * self.da_head_dim
    @property
    def gla_dk(self): return self.d_model // 2 // self.gla_heads
    @property
    def gla_dv(self): return self.d_model // self.gla_heads
    @property
    def gla_qk_w(self): return self.gla_heads * self.gla_dk
    @property
    def gla_v_w(self): return self.gla_heads * self.gla_dv
    @property
    def n_experts(self): return self.peer_n_keys * self.peer_n_keys


def _dot(a, b):
    return jnp.dot(a, b, preferred_element_type=F32)


def _dot_nt(a, b):
    return lax.dot_general(a, b, (((1,), (1,)), ((), ())), preferred_element_type=F32)


def _split_bf16(t):
    hi = t.astype(BF16)
    lo = (t - hi.astype(F32)).astype(BF16)
    return hi, lo


def _params(n_axes, arbitrary_last=False):
    sem = ["parallel"] * n_axes
    if arbitrary_last:
        sem[-1] = "arbitrary"
    return pltpu.CompilerParams(dimension_semantics=tuple(sem), vmem_limit_bytes=VMEM_LIMIT)


def _rope_table_kernel(pos_ref, freq_ref, sign_ref, cos_ref, sin_ref):
    ang = pos_ref[...] * freq_ref[...]
    cos_ref[...] = jnp.cos(ang)
    sin_ref[...] = jnp.sin(ang) * sign_ref[...]


def _rope_tables(cfg, positions):
    half = cfg.da_head_dim // 2
    r = positions.shape[0]
    lane = np.arange(LANES)
    inv_freq = cfg.rope_theta ** (-jnp.arange(half, dtype=F32) / half)
    freq = jnp.tile(inv_freq, LANES // half)[None, :]
    sign = jnp.asarray(np.where(lane % cfg.da_head_dim < half, -1.0, 1.0), F32)[None, :]
    pos = jnp.broadcast_to(positions.astype(F32)[:, None], (r, LANES))
    full = lambda shape: pl.BlockSpec(shape, lambda: (0, 0))
    return pl.pallas_call(
        _rope_table_kernel,
        out_shape=(jax.ShapeDtypeStruct((r, LANES), F32),) * 2,
        in_specs=[full((r, LANES)), full((1, LANES)), full((1, LANES))],
        out_specs=(full((r, LANES)),) * 2,
        name="rope_table",
    )(pos, freq, sign)


def _log_sigmoid(x):
    return jnp.minimum(x, 0.0) - jnp.log1p(jnp.exp(-jnp.abs(x)))


def _in_proj_kernel(cfg, x_ref, g1_ref, w_ref, cos_ref, sin_ref, qkg_ref, seg_ref, w2_ref, b2_ref,
                    q_out, k_out, kb_out, v_out, vb_out, gq_out, gk_out, gv_out, la_out, gr_out, ga_out, gb_out):
    d = cfg.d_model
    qw, gqk, gvw = cfg.da_w, cfg.gla_qk_w, cfg.gla_v_w
    x = x_ref[...]
    xn = (x * lax.rsqrt(jnp.mean(x * x, axis=-1, keepdims=True) + cfg.eps) * g1_ref[...]).astype(BF16)

    widths = (("q", qw), ("k", qw), ("v", qw), ("gq", gqk), ("gk", gqk), ("gv", gvw), ("gr", gvw),
              ("ga", d), ("gb", d), ("glr", LANES))
    cols, start = {}, 0
    for name, width in widths:
        cols[name] = slice(start, start + width)
        start += width

    def proj(name):
        return _dot(xn, w_ref[:, cols[name]])

    nrep = qw // LANES
    cos = jnp.tile(cos_ref[...], (1, nrep))
    sin = jnp.tile(sin_ref[...], (1, nrep))
    lane = lax.broadcasted_iota(jnp.int32, (x.shape[0], qw), 1)
    first_half = (lane & (cfg.da_head_dim - 1)) < (cfg.da_head_dim // 2)
    seg = seg_ref[...]
    lane128 = lax.broadcasted_iota(jnp.int32, (x.shape[0], LANES), 1)

    def group_sums(t):
        hi, lo = _split_bf16(t * t)
        return _dot(hi, seg) + _dot(lo, seg)

    def qk_norm_rope(t, ss, gain):
        r = lax.rsqrt(ss * (1.0 / cfg.da_head_dim) + cfg.eps)
        per_tile = LANES // cfg.da_head_dim
        spread = []
        for v in range(nrep):
            tile = r[:, v * per_tile:v * per_tile + 1]
            for g in range(1, per_tile):
                col = v * per_tile + g
                tile = jnp.where(lane128 < g * cfg.da_head_dim, tile, r[:, col:col + 1])
            spread.append(jnp.broadcast_to(tile, (t.shape[0], LANES)))
        y = t * jnp.concatenate(spread, axis=1) * gain
        half = cfg.da_head_dim // 2
        partner = jnp.where(first_half, pltpu.roll(y, qw - half, 1), pltpu.roll(y, half, 1))
        return y * cos + partner * sin

    q_raw = proj("q")
    q = qk_norm_rope(q_raw, group_sums(q_raw), qkg_ref[0:1, :])
    q_out[...] = (q * (cfg.da_head_dim ** -0.5 * math.log2(math.e))).astype(BF16)
    k_raw = proj("k")
    k = qk_norm_rope(k_raw, group_sums(k_raw), qkg_ref[1:2, :])
    for hm in range(qw // cfg.da_head_dim):
        k_out[:, hm // 2, hm % 2, :] = k[:, hm * cfg.da_head_dim:(hm + 1) * cfg.da_head_dim]
    kb_out[...] = k.astype(BF16)
    v = proj("v")
    v_out[...] = v
    vb_out[...] = v.astype(BF16)
    gq_out[...] = proj("gq") * (cfg.gla_dk ** -0.5)
    gk_out[...] = proj("gk")
    gv_out[...] = proj("gv")
    gr = proj("gr")
    gr_out[...] = gr * jax.nn.sigmoid(gr)
    ga_out[...] = jax.nn.sigmoid(proj("ga"))
    gb_out[...] = jax.nn.sigmoid(proj("gb"))
    gate = _dot(proj("glr").astype(BF16), w2_ref[...]) + b2_ref[...]
    la_out[...] = _log_sigmoid(gate) * (1.0 / cfg.gla_gate_tau)


def _in_proj(cfg, x, cos, sin, p, tm):
    t, d = x.shape
    qw, gqk, gvw = cfg.da_w, cfg.gla_qk_w, cfg.gla_v_w
    n_tab = cos.shape[0] // tm
    tok = lambda w: pl.BlockSpec((tm, w), lambda i: (i, 0))
    const = lambda a: pl.BlockSpec(a.shape, lambda i: (0,) * a.ndim, pipeline_mode=pl.Buffered(1))
    tab = pl.BlockSpec((tm, LANES), lambda i: (i % n_tab, 0))
    consts = [p["g1"], p["w_all"]]
    consts2 = [p["qk_gain"], p["seg"], p["w_gate2"], p["b_gate"]]
    outs = [(qw, BF16), (qw, F32), (qw, BF16), (qw, F32), (qw, BF16), (gqk, F32), (gqk, F32), (gvw, F32),
            (gqk, F32), (gvw, F32), (d, F32), (d, F32)]
    cache_shape = {1: (t, cfg.da_heads, 2, cfg.da_head_dim)}
    cache_spec = {n: pl.BlockSpec((tm,) + s[1:], lambda i, nd=len(s): (i,) + (0,) * (nd - 1))
                  for n, s in cache_shape.items()}
    return pl.pallas_call(
        functools.partial(_in_proj_kernel, cfg),
        grid=(t // tm,),
        in_specs=[tok(d)] + [const(a) for a in consts] + [tab, tab] + [const(a) for a in consts2],
        out_specs=[cache_spec.get(n, tok(w)) for n, (w, _) in enumerate(outs)],
        out_shape=[jax.ShapeDtypeStruct(cache_shape.get(n, (t, w)), dt) for n, (w, dt) in enumerate(outs)],
        compiler_params=_params(1),
        name="in_proj",
    )(x, *consts, cos, sin, *consts2)


def _diff_attn_kernel(cfg, past, tq, tk, n_q, lam_init, q_ref, *refs):
    qi = pl.program_id(2)
    hd = cfg.da_head_dim
    q = q_ref[0]
    lane = lax.broadcasted_iota(jnp.int32, q.shape, 1)
    zero = jnp.zeros_like(q)
    qs = jnp.concatenate([jnp.where(lane < hd, q, zero), jnp.where(lane >= hd, q, zero)], axis=0)
    attend = functools.partial(_attend, cfg, tq, tk, lam_init, refs, qs)

    def blocks(q_start):
        n_blocks = (q_start + tq + tk - 1) // tk
        n_full = (q_start + cfg.chunk) // tk
        return n_full, n_blocks

    if n_q <= ATTN_MAX_STATIC_Q_BLOCKS:
        for c in range(n_q):
            pl.when(qi == c)(functools.partial(attend, past + c * tq, *blocks(past + c * tq), True))
    else:
        q_start = past + qi * tq
        attend(q_start, *blocks(q_start), False)


def _attend(cfg, tq, tk, lam_init, refs, qs, q_start, n_full, n_blocks, static):
    k_ref, v_ref, lamp_ref, g_ref, o_ref, s_ref, mpart_ref, mrow_ref, lpart_ref, acc_ref = refs
    shift = int(math.log2(cfg.chunk))
    n_lane_chunks = tk // LANES
    mpart_ref[...] = jnp.full(mpart_ref.shape, -jnp.inf, F32)

    def scores(j, masked):
        ks = j * tk if static else pl.multiple_of(j * tk, tk)
        s = _dot_nt(qs, k_ref[0, pl.ds(ks, tk), :])
        if masked:
            row = lax.broadcasted_iota(jnp.int32, s.shape, 0)
            col = lax.broadcasted_iota(jnp.int32, s.shape, 1)
            qpos = q_start + jnp.where(row >= tq, row - tq, row)
            allowed = lax.shift_right_logical(ks + col, shift) <= lax.shift_right_logical(qpos, shift)
            s = jnp.where(allowed, s, -jnp.inf)
        s_ref[j] = s
        part = s[:, 0:LANES]
        for c in range(1, n_lane_chunks):
            part = jnp.maximum(part, s[:, c * LANES:(c + 1) * LANES])
        mpart_ref[...] = jnp.maximum(mpart_ref[...], part)

    def in_pairs(fn, lo, hi):
        def pair(n, carry):
            fn(lo + 2 * n)
            fn(lo + 2 * n + 1)
            return carry

        lax.fori_loop(0, (hi - lo) // 2, pair, 0)

        @pl.when((hi - lo) % 2 == 1)
        def _():
            fn(hi - 1)

    if static:
        for j in range(n_blocks):
            scores(j, j >= n_full)
    else:
        in_pairs(lambda j: scores(j, False), 0, n_full)
        in_pairs(lambda j: scores(j, True), n_full, n_blocks)

    mrow_ref[...] = jnp.broadcast_to(jnp.max(mpart_ref[...], axis=-1, keepdims=True), mrow_ref.shape)
    lpart_ref[...] = jnp.zeros(lpart_ref.shape, F32)
    acc_ref[...] = jnp.zeros(acc_ref.shape, F32)

    def weigh(j, carry):
        ks = j * tk if static else pl.multiple_of(j * tk, tk)
        m = mrow_ref[...]
        ps = [jnp.exp2(s_ref[j, :, c * LANES:(c + 1) * LANES] - m) for c in range(n_lane_chunks)]
        lsum = ps[0]
        for pc in ps[1:]:
            lsum = lsum + pc
        lpart_ref[...] += lsum
        p = jnp.concatenate([pc.astype(BF16) for pc in ps], axis=1)
        acc_ref[...] += _dot(p, v_ref[0, pl.ds(ks, tk), :])
        return carry

    if static:
        for j in range(n_blocks):
            weigh(j, 0)
    else:
        in_pairs(lambda j: weigh(j, 0), 0, n_blocks)

    lp = lamp_ref[...]
    lam = (jnp.exp(jnp.sum(lp[0:1] * lp[1:2], axis=-1, keepdims=True))
           - jnp.exp(jnp.sum(lp[2:3] * lp[3:4], axis=-1, keepdims=True)) + lam_init)
    o = acc_ref[...] / jnp.sum(lpart_ref[...], axis=-1, keepdims=True)
    o = o[:tq] - lam * o[tq:]
    o = o * lax.rsqrt(jnp.mean(o * o, axis=-1, keepdims=True) + cfg.eps) * g_ref[...] * (1.0 - lam_init)
    o_ref[0] = o.astype(BF16)


def _diff_attn(cfg, q, k, v, lam_p, subln_g, past, tq, tk, lam_init):
    b, lq, w = q.shape
    lk = k.shape[1]
    vd = cfg.da_v_dim
    assert lq % tq == 0 and lk % tk == 0 and tk % LANES == 0 and tq % cfg.chunk == 0 and past % cfg.chunk == 0
    assert past + lq <= lk
    kv_spec = pl.BlockSpec((1, lk, vd), lambda bi, h, qi: (bi, 0, h))
    q_spec = pl.BlockSpec((1, tq, vd), lambda bi, h, qi: (bi, qi, h))
    return pl.pallas_call(
        functools.partial(_diff_attn_kernel, cfg, past, tq, tk, lq // tq, lam_init),
        grid=(b, cfg.da_heads, lq // tq),
        in_specs=[q_spec, kv_spec, kv_spec,
                  pl.BlockSpec(lam_p.shape, lambda bi, h, qi: (0, 0)),
                  pl.BlockSpec(subln_g.shape, lambda bi, h, qi: (0, 0))],
        out_specs=q_spec,
        out_shape=jax.ShapeDtypeStruct((b, lq, w), BF16),
        scratch_shapes=[pltpu.VMEM((lk // tk, 2 * tq, tk), F32), pltpu.VMEM((2 * tq, LANES), F32),
                        pltpu.VMEM((2 * tq, LANES), F32), pltpu.VMEM((2 * tq, LANES), F32),
                        pltpu.VMEM((2 * tq, vd), F32)],
        compiler_params=_params(3),
        name="diff_attn",
    )(q, k, v, lam_p, subln_g)


def _gla_constants(c):
    r = np.arange(c)
    i = r[:, None]
    rr = r[None, :]
    mats = [(rr <= i), (rr > i)]
    levels = []
    s = c // 2
    while s >= 1:
        levels.append(s)
        s //= 2
    for s in levels:
        bd = (i // s) * s
        mats.append((rr > bd) & (rr <= i))
        bd2 = (i // s + 1) * s
        mats.append((rr > i) & (rr <= bd2))
    stack = np.concatenate([m.astype(np.float32) for m in mats], axis=0)
    lev = np.full((c, c), -1, np.int32)
    jj = r[None, :]
    for n, s in enumerate(levels):
        sel = (jj < i) & (i // s != jj // s) & (lev < 0)
        lev[sel] = n
    lev[i == jj] = len(levels)
    return stack, lev, len(levels)


def _gla_kernel(cfg, n_levels, n_sub, q_ref, k_ref, v_ref, a_ref, s0_ref, mat_ref, lev_ref, g_ref,
                o_ref, s_out_ref, st_ref):
    c, dk, dv = cfg.chunk, cfg.gla_dk, cfg.gla_dv
    step_i = pl.program_id(1)

    @pl.when(step_i == 0)
    def _():
        for hd in range(cfg.gla_heads):
            st_ref[hd] = s0_ref[0, hd].T

    mat = mat_ref[...]
    lev = lev_ref[...]

    heads = range(cfg.gla_heads)
    kcols = [slice(hd * dk, (hd + 1) * dk) for hd in heads]
    vcols = [slice(hd * dv, (hd + 1) * dv) for hd in heads]

    def chunk(n, carry):
        rows = pl.ds(pl.multiple_of(n * c, c), c)
        a_hi, a_lo = _split_bf16(a_ref[0, rows, :])
        e_all = _dot(mat, a_hi) + _dot(mat, a_lo)
        e = [e_all[:, kcols[hd]] for hd in heads]
        q = [q_ref[0, rows, kcols[hd]] for hd in heads]
        k = [k_ref[0, rows, kcols[hd]] for hd in heads]
        vb = [v_ref[0, rows, vcols[hd]].astype(BF16) for hd in heads]

        att = [jnp.where(lev == n_levels, _dot_nt(q[hd].astype(BF16), k[hd].astype(BF16)), 0.0) for hd in heads]
        for n_l in range(n_levels):
            eq = slice((2 + 2 * n_l) * c, (3 + 2 * n_l) * c)
            ek = slice((3 + 2 * n_l) * c, (4 + 2 * n_l) * c)
            a_l = [_dot_nt((q[hd] * jnp.exp(e[hd][eq])).astype(BF16), (k[hd] * jnp.exp(e[hd][ek])).astype(BF16))
                   for hd in heads]
            att = [jnp.where(lev == n_l, a_l[hd], att[hd]) for hd in heads]

        st = [st_ref[hd] for hd in heads]
        o = [_dot_nt((q[hd] * jnp.exp(e[hd][0:c])).astype(BF16), st[hd].astype(BF16))
             + _dot(att[hd].astype(BF16), vb[hd]) for hd in heads]
        kl = [(k[hd] * jnp.exp(e[hd][c:2 * c])).astype(BF16) for hd in heads]
        upd = [lax.dot_general(vb[hd], kl[hd], (((0,), (0,)), ((), ())), preferred_element_type=F32)
               for hd in heads]
        for hd in heads:
            st_ref[hd] = st[hd] * jnp.exp(e[hd][c - 1:c, :]) + upd[hd]
            on = o[hd] * lax.rsqrt(jnp.mean(o[hd] * o[hd], axis=-1, keepdims=True) + cfg.eps) * g_ref[...]
            o_ref[0, rows, vcols[hd]] = on
        return carry

    lax.fori_loop(0, n_sub, chunk, 0, unroll=2 if n_sub % 2 == 0 else 1)

    @pl.when(step_i == pl.num_programs(1) - 1)
    def _():
        for hd in range(cfg.gla_heads):
            s_out_ref[0, hd] = st_ref[hd].T


def _gla(cfg, gq, gk, gv, la, s0, norm_g, lb):
    b, l, _ = gq.shape
    c, dk, dv, h = cfg.chunk, cfg.gla_dk, cfg.gla_dv, cfg.gla_heads
    assert l % lb == 0 and lb % c == 0
    stack, lev, n_levels = _gla_constants(c)
    mat = jnp.asarray(stack, BF16)
    lev = jnp.asarray(lev)
    seq = lambda w: pl.BlockSpec((1, lb, h * w), lambda bi, si: (bi, si, 0))
    state = pl.BlockSpec((1, h, dk, dv), lambda bi, si: (bi, 0, 0, 0))
    const = lambda a: pl.BlockSpec(a.shape, lambda bi, si: (0,) * a.ndim)
    return pl.pallas_call(
        functools.partial(_gla_kernel, cfg, n_levels, lb // c),
        grid=(b, l // lb),
        in_specs=[seq(dk), seq(dk), seq(dv), seq(dk), state, const(mat), const(lev), const(norm_g)],
        out_specs=[seq(dv), state],
        out_shape=[jax.ShapeDtypeStruct((b, l, h * dv), F32), jax.ShapeDtypeStruct((b, h, dk, dv), F32)],
        scratch_shapes=[pltpu.VMEM((h, dv, dk), F32)],
        compiler_params=_params(2, arbitrary_last=True),
        name="gla",
    )(gq, gk, gv, la, s0, mat, lev, norm_g)


def _merge_kernel(cfg, x_ref, oa_ref, ob_ref, gr_ref, ga_ref, gb_ref, wa_ref, wb_ref, wo_ref, g2_ref,
                  h_ref, hn_ref):
    a = _dot(oa_ref[...], wa_ref[...])
    bb = _dot((ob_ref[...] * gr_ref[...]).astype(BF16), wb_ref[...])
    mix = ga_ref[...] * a + gb_ref[...] * bb
    h = x_ref[...] + _dot(mix.astype(BF16), wo_ref[...])
    h_ref[...] = h
    hn = h * lax.rsqrt(jnp.mean(h * h, axis=-1, keepdims=True) + cfg.eps) * g2_ref[...]
    hn_ref[...] = hn.astype(BF16)


def _merge_out(cfg, x, oa, ob, gr, ga, gb, p, tm):
    t, d = x.shape
    tok = lambda a: pl.BlockSpec((tm, a.shape[1]), lambda i: (i, 0))
    const = lambda a: pl.BlockSpec(a.shape, lambda i: (0,) * a.ndim, pipeline_mode=pl.Buffered(1))
    toks = [x, oa, ob, gr, ga, gb]
    consts = [p["w_a"], p["w_b"], p["w_o"], p["g2"]]
    return pl.pallas_call(
        functools.partial(_merge_kernel, cfg),
        grid=(t // tm,),
        in_specs=[tok(a) for a in toks] + [const(a) for a in consts],
        out_specs=[pl.BlockSpec((tm, d), lambda i: (i, 0))] * 2,
        out_shape=[jax.ShapeDtypeStruct((t, d), F32), jax.ShapeDtypeStruct((t, d), BF16)],
        compiler_params=_params(1),
        name="merge_out",
    )(*toks, *consts)


def _top_values(x, k, with_rank=False):
    vals = []
    rank = jnp.full(x.shape, float(k), F32) if with_rank else None
    for n in range(k):
        m = jnp.max(x, axis=0, keepdims=True)
        vals.append(m)
        hit = x == m
        if with_rank:
            rank = jnp.where(hit, float(n), rank)
        x = jnp.where(hit, -jnp.inf, x)
    return (vals, rank) if with_rank else vals


def _rows_to_block(rows):
    k, t = len(rows), rows[0].shape[1]
    idx = lax.broadcasted_iota(jnp.int32, (k, t), 0)
    out = jnp.zeros((k, t), F32)
    for n, r in enumerate(rows):
        out = jnp.where(idx == n, r, out)
    return out


def _candidate_sums(top0, top1, k):
    tile = 8
    assert k % tile == 0 and k // (tile + 1) == 1
    blk0, blk1 = _rows_to_block(top0), _rows_to_block(top1)
    row = lax.broadcasted_iota(jnp.int32, (tile, top0[0].shape[1]), 0)
    pieces = []
    for b in range(tile):
        n_a = k // (b + 1)
        for a0 in range(0, n_a, tile):
            piece = blk0[a0:a0 + tile] + top1[b]
            pieces.append(piece if a0 + tile <= n_a else jnp.where(row < n_a - a0, piece, -jnp.inf))
    for b0 in range(tile, k, tile):
        pieces.append(blk1[b0:b0 + tile] + top0[0])
    return jnp.concatenate(pieces, axis=0)


def _peer_route_kernel(cfg, hn_ref, wq_ref, keys_ref, c1_ref, r2_ref, e1_ref, e2_ref):
    nk, topk = cfg.peer_n_keys, cfg.peer_topk
    sub = cfg.peer_qdim // 2
    q = _dot(hn_ref[...], wq_ref[...]).astype(BF16)
    for h in range(cfg.peer_heads):
        sc = [_dot_nt(keys_ref[2 * h + half], q[:, (2 * h + half) * sub:(2 * h + half + 1) * sub])
              for half in range(2)]
        top0 = _top_values(sc[0], topk)
        top1, rank1 = _top_values(sc[1], topk, with_rank=True)
        top = [top0, top1]
        best = _top_values(_candidate_sums(top0, top1, topk), topk)
        tau = best[topk - 1]
        z = best[0] * 0.0
        for cv in best:
            z = z + jnp.exp(cv - best[0])
        count = jnp.zeros(sc[0].shape, F32)
        for b in range(topk):
            count = count + jnp.where(sc[0] + top[1][b] >= tau, 1.0, 0.0)
        rows = pl.ds(h * nk, nk)
        c1_ref[rows, :] = count
        r2_ref[rows, :] = rank1
        e1_ref[rows, :] = jnp.exp(sc[0] - top[0][0]) / z
        e2_ref[rows, :] = jnp.exp(sc[1] - top[1][0])


def _peer_route(cfg, hn, wq, keys, tb):
    t, d = hn.shape
    hk = cfg.peer_heads * cfg.peer_n_keys
    const = lambda a: pl.BlockSpec(a.shape, lambda i: (0,) * a.ndim, pipeline_mode=pl.Buffered(1))
    col = lambda r: pl.BlockSpec((r, tb), lambda i: (0, i))
    return pl.pallas_call(
        functools.partial(_peer_route_kernel, cfg),
        grid=(t // tb,),
        in_specs=[pl.BlockSpec((tb, d), lambda i: (i, 0)), const(wq), const(keys)],
        out_specs=[col(hk)] * 4,
        out_shape=[jax.ShapeDtypeStruct((hk, t), F32)] * 4,
        compiler_params=_params(1),
        name="peer_route",
    )(hn, wq, keys)


def _gelu(x):
    return 0.5 * x * (1.0 + lax.erf(x * (2.0 ** -0.5)))


def _peer_dense_kernel(cfg, ib, hn_ref, h_ref, u_ref, vt_ref, c1_ref, r2_ref, e1_ref, e2_ref,
                       y_ref, acc_ref, act_ref, p_ref):
    nk = cfg.peer_n_keys
    ei = pl.program_id(1)
    tb = hn_ref.shape[0]

    @pl.when(ei == 0)
    def _():
        acc_ref[...] = jnp.zeros(acc_ref.shape, F32)

    act_ref[...] = _gelu(_dot_nt(u_ref[...], hn_ref[...]))

    n_row_tiles = nk // GATE_ROWS

    def gate_tile(n, carry):
        r0 = pl.multiple_of((n % n_row_tiles) * GATE_ROWS, GATE_ROWS)
        lanes = pl.ds(pl.multiple_of((n // n_row_tiles) * LANES, LANES), LANES)
        for i0 in range(0, ib, GATE_FIRST):
            g = [None] * GATE_FIRST
            for h in range(cfg.peer_heads):
                rows = pl.ds(h * nk + r0, GATE_ROWS)
                r2, e2 = r2_ref[rows, lanes], e2_ref[rows, lanes]
                for k in range(GATE_FIRST):
                    keep = r2 < c1_ref[h, i0 + k:i0 + k + 1, lanes]
                    term = jnp.where(keep, e2, 0.0) * e1_ref[h, i0 + k:i0 + k + 1, lanes]
                    g[k] = term if h == 0 else g[k] + term
            for k in range(GATE_FIRST):
                rows = pl.ds((i0 + k) * nk + r0, GATE_ROWS)
                p_ref[rows, lanes] = (g[k] * act_ref[rows, lanes]).astype(BF16)
        return carry

    lax.fori_loop(0, n_row_tiles * (tb // LANES), gate_tile, 0)
    acc_ref[...] += lax.dot_general(vt_ref[...], p_ref[...], (((0,), (0,)), ((), ())),
                                    preferred_element_type=F32)

    @pl.when(ei == pl.num_programs(1) - 1)
    def _():
        y_ref[...] = h_ref[...] + acc_ref[...].T


def _peer_dense(cfg, hn, h, u, vt, route, tb, ib):
    t, d = hn.shape
    nk = cfg.peer_n_keys
    eb = ib * nk
    hk = cfg.peer_heads * nk
    c1, r2, e1, e2 = route
    by_head = lambda a: a.reshape(cfg.peer_heads, nk, t)
    tokb = pl.BlockSpec((tb, d), lambda ti, ei: (ti, 0))
    col = lambda r: pl.BlockSpec((r, tb), lambda ti, ei: (0, ti))
    first = pl.BlockSpec((cfg.peer_heads, ib, tb), lambda ti, ei: (0, ei, ti))
    return pl.pallas_call(
        functools.partial(_peer_dense_kernel, cfg, ib),
        grid=(t // tb, cfg.n_experts // eb),
        in_specs=[tokb, tokb,
                  pl.BlockSpec((eb, d), lambda ti, ei: (ei, 0)),
                  pl.BlockSpec((eb, d), lambda ti, ei: (ei, 0)),
                  first, col(hk), first, col(hk)],
        out_specs=tokb,
        out_shape=jax.ShapeDtypeStruct((t, d), F32),
        scratch_shapes=[pltpu.VMEM((d, tb), F32), pltpu.VMEM((eb, tb), F32), pltpu.VMEM((eb, tb), BF16)],
        compiler_params=_params(2, arbitrary_last=True),
        name="peer_dense",
    )(hn, h, u, vt, by_head(c1), r2, by_head(e1), e2)


def _prep_params(cfg, norm1_g, w_in, qk_norm_g, da_lambda, da_subln_g, gla_w_gate2, gla_b_gate, gla_norm_g,
                 w_branch_a, w_branch_b, w_out, norm2_g, peer_w_q, peer_sub_keys, peer_u, peer_v):
    d, qw, gqk, gvw, rank = cfg.d_model, cfg.da_w, cfg.gla_qk_w, cfg.gla_v_w, cfg.gla_gate_rank
    splits = (qw, qw, qw, gqk, gqk, gvw, rank, gvw, d, d)
    starts = np.concatenate([[0], np.cumsum(splits)])
    piece = lambda n: w_in[:, starts[n]:starts[n + 1]]
    glr = jnp.pad(piece(6), ((0, 0), (0, LANES - rank)))
    w_all = jnp.concatenate([piece(n) for n in (0, 1, 2, 3, 4, 5, 7, 8, 9)] + [glr], axis=1).astype(BF16)
    group = np.arange(qw) // cfg.da_head_dim
    seg = (group[:, None] == np.arange(LANES)[None, :]).astype(np.float32)
    return {
        "g1": norm1_g[None, :], "w_all": w_all,
        "qk_gain": jnp.tile(qk_norm_g, (1, qw // cfg.da_head_dim)),
        "seg": jnp.asarray(seg, BF16),
        "w_gate2": jnp.pad(gla_w_gate2, ((0, LANES - rank), (0, 0))).astype(BF16),
        "b_gate": gla_b_gate[None, :],
        "da_lambda": da_lambda, "subln_g": da_subln_g[None, :], "gla_norm_g": gla_norm_g[None, :],
        "w_a": w_branch_a.astype(BF16), "w_b": w_branch_b.astype(BF16), "w_o": w_out.astype(BF16),
        "g2": norm2_g[None, :],
        "wq": peer_w_q.astype(BF16),
        "keys": peer_sub_keys.reshape(cfg.peer_heads * 2, cfg.peer_n_keys, cfg.peer_qdim // 2).astype(BF16),
        "u": peer_u.astype(BF16), "vt": peer_v.astype(BF16),
    }


def _pick(n, target):
    t = min(n, target)
    while n % t:
        t //= 2
    return t


def _layer(cfg, x, cache_k, cache_v, s0, p, lam_init):
    b, l, d = x.shape
    t = b * l
    past = 0 if cache_k is None else cache_k.shape[1]
    qw = cfg.da_w

    tm = _pick(t, 256)
    if l % tm == 0:
        positions = past + jnp.arange(l)
    else:
        positions = past + (jnp.arange(tm) % l)
    cos, sin = _rope_tables(cfg, positions)
    q, k, kb, v, vb, gq, gk, gv, la, gr, ga, gb = _in_proj(cfg, x.reshape(t, d), cos, sin, p, tm)

    r3 = lambda a: a.reshape(b, l, a.shape[-1])
    if cache_k is None:
        kk, vv = r3(kb), r3(vb)
        tq = _pick(l, ATTN_TQ)
        tk = _pick(l, ATTN_TK)
    else:
        tq = _pick(l, ATTN_TQ)
        tk = -(-(past + l) // LANES) * LANES
        pad = jnp.zeros((b, tk - past - l, qw), BF16)
        kk = jnp.concatenate([cache_k.reshape(b, past, qw).astype(BF16), r3(kb), pad], axis=1)
        vv = jnp.concatenate([cache_v.reshape(b, past, qw).astype(BF16), r3(vb), pad], axis=1)
    oa = _diff_attn(cfg, r3(q), kk, vv, p["da_lambda"], p["subln_g"], past, tq, tk, lam_init)

    if s0 is None:
        s0 = jnp.zeros((b, cfg.gla_heads, cfg.gla_dk, cfg.gla_dv), F32)
    ob, s_new = _gla(cfg, r3(gq), r3(gk), r3(gv), r3(la), s0, p["gla_norm_g"], _pick(l, 512))

    h, hn = _merge_out(cfg, x.reshape(t, d), oa.reshape(t, qw), ob.reshape(t, -1), gr, ga, gb, p, tm)

    tb = _pick(t, 512)
    route = _peer_route(cfg, hn, p["wq"], p["keys"], _pick(t, ROUTE_TOKENS))
    y = _peer_dense(cfg, hn, h, p["u"], p["vt"], route, tb, PEER_FIRST_PER_STEP)

    k_new = k.reshape(b, l, cfg.da_heads, 2, cfg.da_head_dim)
    v_new = v.reshape(b, l, cfg.da_heads, cfg.da_v_dim)
    return y.reshape(b, l, d), k_new, v_new, s_new


def _forward(cfg, x_prompt, x_sample, cache_k, cache_v, state_gla, *weights):
    yp, ys = x_prompt, x_sample
    outs = [[] for _ in range(6)]
    for layer in range(cache_k.shape[0]):
        p = _prep_params(cfg, *[w[layer] for w in weights])
        lam_init = 0.8 - 0.6 * math.exp(-0.3 * layer)
        yp, kp, vp, sp = _layer(cfg, yp, None, None, None, p, lam_init)
        ys, kn, vn, sn = _layer(cfg, ys, cache_k[layer], cache_v[layer], state_gla[layer], p, lam_init)
        for lst, a in zip(outs, (kp, vp, sp, kn, vn, sn)):
            lst.append(a)
    return (yp, ys) + tuple(jnp.stack(o) for o in outs)


def kernel(x_prompt, x_sample, cache_k, cache_v, state_gla, norm1_g, w_in, qk_norm_g, da_lambda, da_subln_g,
           gla_w_gate2, gla_b_gate, gla_norm_g, w_branch_a, w_branch_b, w_out, norm2_g, peer_w_q,
           peer_sub_keys, peer_u, peer_v):
    return _forward(Cfg(), x_prompt, x_sample, cache_k, cache_v, state_gla, norm1_g, w_in, qk_norm_g, da_lambda,
                    da_subln_g, gla_w_gate2, gla_b_gate, gla_norm_g, w_branch_a, w_branch_b, w_out, norm2_g,
                    peer_w_q, peer_sub_keys, peer_u, peer_v)
```

```python
import functools
import math
from typing import NamedTuple

import jax
import jax.numpy as jnp
import numpy as np
from jax import lax
from jax.experimental import pallas as pl
from jax.experimental.pallas import tpu as pltpu

F32 = jnp.float32
BF16 = jnp.bfloat16
LANES = 128
VMEM_LIMIT = 56 * 1024 * 1024
GATE_ROWS = 16
ROUTE_TOKENS = 256
PEER_FIRST_PER_STEP = 16
GATE_FIRST = 8
ATTN_TQ = 512
ATTN_TK = 512
ATTN_MAX_STATIC_Q_BLOCKS = 1


class Cfg(NamedTuple):
    d_model: int = 1024
    chunk: int = 64
    eps: float = 1e-6
    da_heads: int = 8
    da_head_dim: int = 64
    rope_theta: float = 10000.0
    gla_heads: int = 4
    gla_gate_rank: int = 16
    gla_gate_tau: float = 16.0
    peer_heads: int = 8
    peer_n_keys: int = 128
    peer_qdim: int = 256
    peer_topk: int = 16

    @property
    def da_v_dim(self): return 2 * self.da_head_dim
    @property
    def da_w(self): return self.da_heads * 2 * self.da_head_dim
    @property
    def gla_dk(self): return self.d_model // 2 // self.gla_heads
    @property
    def gla_dv(self): return self.d_model // self.gla_heads
    @property
    def gla_qk_w(self): return self.gla_heads * self.gla_dk
    @property
    def gla_v_w(self): return self.gla_heads * self.gla_dv
    @property
    def n_experts(self): return self.peer_n_keys * self.peer_n_keys


def _dot(a, b):
    return jnp.dot(a, b, preferred_element_type=F32)


def _dot_nt(a, b):
    return lax.dot_general(a, b, (((1,), (1,)), ((), ())), preferred_element_type=F32)


def _split_bf16(t):
    hi = t.astype(BF16)
    lo = (t - hi.astype(F32)).astype(BF16)
    return hi, lo


def _params(n_axes, arbitrary_last=False):
    sem = ["parallel"] * n_axes
    if arbitrary_last:
        sem[-1] = "arbitrary"
    return pltpu.CompilerParams(dimension_semantics=tuple(sem), vmem_limit_bytes=VMEM_LIMIT)


def _rope_table_kernel(pos_ref, freq_ref, sign_ref, cos_ref, sin_ref):
    ang = pos_ref[...] * freq_ref[...]
    cos_ref[...] = jnp.cos(ang)
    sin_ref[...] = jnp.sin(ang) * sign_ref[...]


def _rope_tables(cfg, positions):
    half = cfg.da_head_dim // 2
    r = positions.shape[0]
    lane = np.arange(LANES)
    inv_freq = cfg.rope_theta ** (-jnp.arange(half, dtype=F32) / half)
    freq = jnp.tile(inv_freq, LANES // half)[None, :]
    sign = jnp.asarray(np.where(lane % cfg.da_head_dim < half, -1.0, 1.0), F32)[None, :]
    pos = jnp.broadcast_to(positions.astype(F32)[:, None], (r, LANES))
    full = lambda shape: pl.BlockSpec(shape, lambda: (0, 0))
    return pl.pallas_call(
        _rope_table_kernel,
        out_shape=(jax.ShapeDtypeStruct((r, LANES), F32),) * 2,
        in_specs=[full((r, LANES)), full((1, LANES)), full((1, LANES))],
        out_specs=(full((r, LANES)),) * 2,
        name="rope_table",
    )(pos, freq, sign)


def _log_sigmoid(x):
    return jnp.minimum(x, 0.0) - jnp.log1p(jnp.exp(-jnp.abs(x)))


def _in_proj_kernel(cfg, x_ref, g1_ref, w_ref, cos_ref, sin_ref, qkg_ref, seg_ref, w2_ref, b2_ref,
                    q_out, k_out, kb_out, v_out, vb_out, gq_out, gk_out, gv_out, la_out, gr_out, ga_out, gb_out):
    d = cfg.d_model
    qw, gqk, gvw = cfg.da_w, cfg.gla_qk_w, cfg.gla_v_w
    x = x_ref[...]
    xn = (x * lax.rsqrt(jnp.mean(x * x, axis=-1, keepdims=True) + cfg.eps) * g1_ref[...]).astype(BF16)

    widths = (("q", qw), ("k", qw), ("v", qw), ("gq", gqk), ("gk", gqk), ("gv", gvw), ("gr", gvw),
              ("ga", d), ("gb", d), ("glr", LANES))
    cols, start = {}, 0
    for name, width in widths:
        cols[name] = slice(start, start + width)
        start += width

    def proj(name):
        return _dot(xn, w_ref[:, cols[name]])

    nrep = qw // LANES
    cos = jnp.tile(cos_ref[...], (1, nrep))
    sin = jnp.tile(sin_ref[...], (1, nrep))
    lane = lax.broadcasted_iota(jnp.int32, (x.shape[0], qw), 1)
    first_half = (lane & (cfg.da_head_dim - 1)) < (cfg.da_head_dim // 2)
    seg = seg_ref[...]
    lane128 = lax.broadcasted_iota(jnp.int32, (x.shape[0], LANES), 1)

    def group_sums(t):
        hi, lo = _split_bf16(t * t)
        return _dot(hi, seg) + _dot(lo, seg)

    def qk_norm_rope(t, ss, gain):
        r = lax.rsqrt(ss * (1.0 / cfg.da_head_dim) + cfg.eps)
        per_tile = LANES // cfg.da_head_dim
        spread = []
        for v in range(nrep):
            tile = r[:, v * per_tile:v * per_tile + 1]
            for g in range(1, per_tile):
                col = v * per_tile + g
                tile = jnp.where(lane128 < g * cfg.da_head_dim, tile, r[:, col:col + 1])
            spread.append(jnp.broadcast_to(tile, (t.shape[0], LANES)))
        y = t * jnp.concatenate(spread, axis=1) * gain
        half = cfg.da_head_dim // 2
        partner = jnp.where(first_half, pltpu.roll(y, qw - half, 1), pltpu.roll(y, half, 1))
        return y * cos + partner * sin

    q_raw = proj("q")
    q = qk_norm_rope(q_raw, group_sums(q_raw), qkg_ref[0:1, :])
    q_out[...] = (q * (cfg.da_head_dim ** -0.5 * math.log2(math.e))).astype(BF16)
    k_raw = proj("k")
    k = qk_norm_rope(k_raw, group_sums(k_raw), qkg_ref[1:2, :])
    for hm in range(qw // cfg.da_head_dim):
        k_out[:, hm // 2, hm % 2, :] = k[:, hm * cfg.da_head_dim:(hm + 1) * cfg.da_head_dim]
    kb_out[...] = k.astype(BF16)
    v = proj("v")
    v_out[...] = v
    vb_out[...] = v.astype(BF16)
    gq_out[...] = proj("gq") * (cfg.gla_dk ** -0.5)
    gk_out[...] = proj("gk")
    gv_out[...] = proj("gv")
    gr = proj("gr")
    gr_out[...] = gr * jax.nn.sigmoid(gr)
    ga_out[...] = jax.nn.sigmoid(proj("ga"))
    gb_out[...] = jax.nn.sigmoid(proj("gb"))
    gate = _dot(proj("glr").astype(BF16), w2_ref[...]) + b2_ref[...]
    la_out[...] = _log_sigmoid(gate) * (1.0 / cfg.gla_gate_tau)


def _in_proj(cfg, x, cos, sin, p, tm):
    t, d = x.shape
    qw, gqk, gvw = cfg.da_w, cfg.gla_qk_w, cfg.gla_v_w
    n_tab = cos.shape[0] // tm
    tok = lambda w: pl.BlockSpec((tm, w), lambda i: (i, 0))
    const = lambda a: pl.BlockSpec(a.shape, lambda i: (0,) * a.ndim, pipeline_mode=pl.Buffered(1))
    tab = pl.BlockSpec((tm, LANES), lambda i: (i % n_tab, 0))
    consts = [p["g1"], p["w_all"]]
    consts2 = [p["qk_gain"], p["seg"], p["w_gate2"], p["b_gate"]]
    outs = [(qw, BF16), (qw, F32), (qw, BF16), (qw, F32), (qw, BF16), (gqk, F32), (gqk, F32), (gvw, F32),
            (gqk, F32), (gvw, F32), (d, F32), (d, F32)]
    cache_shape = {1: (t, cfg.da_heads, 2, cfg.da_head_dim)}
    cache_spec = {n: pl.BlockSpec((tm,) + s[1:], lambda i, nd=len(s): (i,) + (0,) * (nd - 1))
                  for n, s in cache_shape.items()}
    return pl.pallas_call(
        functools.partial(_in_proj_kernel, cfg),
        grid=(t // tm,),
        in_specs=[tok(d)] + [const(a) for a in consts] + [tab, tab] + [const(a) for a in consts2],
        out_specs=[cache_spec.get(n, tok(w)) for n, (w, _) in enumerate(outs)],
        out_shape=[jax.ShapeDtypeStruct(cache_shape.get(n, (t, w)), dt) for n, (w, dt) in enumerate(outs)],
        compiler_params=_params(1),
        name="in_proj",
    )(x, *consts, cos, sin, *consts2)


def _diff_attn_kernel(cfg, past, tq, tk, n_q, heads_per_step, lam_init,
                      q_ref, k_ref, v_ref, lamp_ref, g_ref, o_ref, *scratch):
    qi = pl.program_id(2)
    hd, vd = cfg.da_head_dim, cfg.da_v_dim

    def blocks(q_start):
        n_blocks = (q_start + tq + tk - 1) // tk
        n_full = (q_start + cfg.chunk) // tk
        return n_full, n_blocks

    for hh in range(heads_per_step):
        cols = slice(hh * vd, (hh + 1) * vd)
        q = q_ref[0, :, cols]
        lane = lax.broadcasted_iota(jnp.int32, q.shape, 1)
        zero = jnp.zeros_like(q)
        qs = jnp.concatenate([jnp.where(lane < hd, q, zero), jnp.where(lane >= hd, q, zero)], axis=0)
        refs = (k_ref.at[:, :, cols], v_ref.at[:, :, cols], lamp_ref, g_ref, o_ref.at[:, :, cols]) + scratch
        attend = functools.partial(_attend, cfg, tq, tk, lam_init, refs, qs)
        if n_q <= ATTN_MAX_STATIC_Q_BLOCKS:
            for c in range(n_q):
                pl.when(qi == c)(functools.partial(attend, past + c * tq, *blocks(past + c * tq), True))
        else:
            q_start = past + qi * tq
            attend(q_start, *blocks(q_start), False)


def _attend(cfg, tq, tk, lam_init, refs, qs, q_start, n_full, n_blocks, static):
    k_ref, v_ref, lamp_ref, g_ref, o_ref, s_ref, mpart_ref, mrow_ref, lpart_ref, acc_ref = refs
    shift = int(math.log2(cfg.chunk))
    n_lane_chunks = tk // LANES
    mpart_ref[...] = jnp.full(mpart_ref.shape, -jnp.inf, F32)

    def scores(j, masked):
        ks = j * tk if static else pl.multiple_of(j * tk, tk)
        s = _dot_nt(qs, k_ref[0, pl.ds(ks, tk), :])
        if masked:
            row = lax.broadcasted_iota(jnp.int32, s.shape, 0)
            col = lax.broadcasted_iota(jnp.int32, s.shape, 1)
            qpos = q_start + jnp.where(row >= tq, row - tq, row)
            allowed = lax.shift_right_logical(ks + col, shift) <= lax.shift_right_logical(qpos, shift)
            s = jnp.where(allowed, s, -jnp.inf)
        s_ref[j] = s
        part = s[:, 0:LANES]
        for c in range(1, n_lane_chunks):
            part = jnp.maximum(part, s[:, c * LANES:(c + 1) * LANES])
        mpart_ref[...] = jnp.maximum(mpart_ref[...], part)

    def in_pairs(fn, lo, hi):
        def pair(n, carry):
            fn(lo + 2 * n)
            fn(lo + 2 * n + 1)
            return carry

        lax.fori_loop(0, (hi - lo) // 2, pair, 0)

        @pl.when((hi - lo) % 2 == 1)
        def _():
            fn(hi - 1)

    if static:
        for j in range(n_blocks):
            scores(j, j >= n_full)
    else:
        in_pairs(lambda j: scores(j, False), 0, n_full)
        in_pairs(lambda j: scores(j, True), n_full, n_blocks)

    mrow_ref[...] = jnp.broadcast_to(jnp.max(mpart_ref[...], axis=-1, keepdims=True), mrow_ref.shape)
    lpart_ref[...] = jnp.zeros(lpart_ref.shape, F32)
    acc_ref[...] = jnp.zeros(acc_ref.shape, F32)

    def weigh(j, carry):
        ks = j * tk if static else pl.multiple_of(j * tk, tk)
        m = mrow_ref[...]
        ps = [jnp.exp2(s_ref[j, :, c * LANES:(c + 1) * LANES] - m) for c in range(n_lane_chunks)]
        lsum = ps[0]
        for pc in ps[1:]:
            lsum = lsum + pc
        lpart_ref[...] += lsum
        p = jnp.concatenate([pc.astype(BF16) for pc in ps], axis=1)
        acc_ref[...] += _dot(p, v_ref[0, pl.ds(ks, tk), :])
        return carry

    if static:
        for j in range(n_blocks):
            weigh(j, 0)
    else:
        in_pairs(lambda j: weigh(j, 0), 0, n_blocks)

    lp = lamp_ref[...]
    lam = (jnp.exp(jnp.sum(lp[0:1] * lp[1:2], axis=-1, keepdims=True))
           - jnp.exp(jnp.sum(lp[2:3] * lp[3:4], axis=-1, keepdims=True)) + lam_init)
    o = acc_ref[...] / jnp.sum(lpart_ref[...], axis=-1, keepdims=True)
    o = o[:tq] - lam * o[tq:]
    o = o * lax.rsqrt(jnp.mean(o * o, axis=-1, keepdims=True) + cfg.eps) * g_ref[...] * (1.0 - lam_init)
    o_ref[0] = o.astype(BF16)


def _diff_attn(cfg, q, k, v, lam_p, subln_g, past, tq, tk, lam_init):
    b, lq, w = q.shape
    lk = k.shape[1]
    vd = cfg.da_v_dim
    assert lq % tq == 0 and lk % tk == 0 and tk % LANES == 0 and tq % cfg.chunk == 0 and past % cfg.chunk == 0
    assert past + lq <= lk
    hps = cfg.da_heads if lq // tq <= ATTN_MAX_STATIC_Q_BLOCKS else 1
    kv_spec = pl.BlockSpec((1, lk, hps * vd), lambda bi, h, qi: (bi, 0, h))
    q_spec = pl.BlockSpec((1, tq, hps * vd), lambda bi, h, qi: (bi, qi, h))
    return pl.pallas_call(
        functools.partial(_diff_attn_kernel, cfg, past, tq, tk, lq // tq, hps, lam_init),
        grid=(b, cfg.da_heads // hps, lq // tq),
        in_specs=[q_spec, kv_spec, kv_spec,
                  pl.BlockSpec(lam_p.shape, lambda bi, h, qi: (0, 0)),
                  pl.BlockSpec(subln_g.shape, lambda bi, h, qi: (0, 0))],
        out_specs=q_spec,
        out_shape=jax.ShapeDtypeStruct((b, lq, w), BF16),
        scratch_shapes=[pltpu.VMEM((lk // tk, 2 * tq, tk), F32), pltpu.VMEM((2 * tq, LANES), F32),
                        pltpu.VMEM((2 * tq, LANES), F32), pltpu.VMEM((2 * tq, LANES), F32),
                        pltpu.VMEM((2 * tq, vd), F32)],
        compiler_params=_params(3),
        name="diff_attn",
    )(q, k, v, lam_p, subln_g)


def _gla_constants(c):
    r = np.arange(c)
    i = r[:, None]
    rr = r[None, :]
    mats = [(rr <= i), (rr > i)]
    levels = []
    s = c // 2
    while s >= 1:
        levels.append(s)
        s //= 2
    for s in levels:
        bd = (i // s) * s
        mats.append((rr > bd) & (rr <= i))
        bd2 = (i // s + 1) * s
        mats.append((rr > i) & (rr <= bd2))
    stack = np.concatenate([m.astype(np.float32) for m in mats], axis=0)
    lev = np.full((c, c), -1, np.int32)
    jj = r[None, :]
    for n, s in enumerate(levels):
        sel = (jj < i) & (i // s != jj // s) & (lev < 0)
        lev[sel] = n
    lev[i == jj] = len(levels)
    return stack, lev, len(levels)


def _gla_kernel(cfg, n_levels, n_sub, q_ref, k_ref, v_ref, a_ref, s0_ref, mat_ref, lev_ref, g_ref,
                o_ref, s_out_ref, st_ref):
    c, dk, dv = cfg.chunk, cfg.gla_dk, cfg.gla_dv
    step_i = pl.program_id(1)

    @pl.when(step_i == 0)
    def _():
        for hd in range(cfg.gla_heads):
            st_ref[hd] = s0_ref[0, hd].T

    mat = mat_ref[...]
    lev = lev_ref[...]

    heads = range(cfg.gla_heads)
    kcols = [slice(hd * dk, (hd + 1) * dk) for hd in heads]
    vcols = [slice(hd * dv, (hd + 1) * dv) for hd in heads]

    def chunk(n, carry):
        rows = pl.ds(pl.multiple_of(n * c, c), c)
        a_hi, a_lo = _split_bf16(a_ref[0, rows, :])
        e_all = _dot(mat, a_hi) + _dot(mat, a_lo)
        e = [e_all[:, kcols[hd]] for hd in heads]
        q = [q_ref[0, rows, kcols[hd]] for hd in heads]
        k = [k_ref[0, rows, kcols[hd]] for hd in heads]
        vb = [v_ref[0, rows, vcols[hd]].astype(BF16) for hd in heads]

        att = [jnp.where(lev == n_levels, _dot_nt(q[hd].astype(BF16), k[hd].astype(BF16)), 0.0) for hd in heads]
        for n_l in range(n_levels):
            eq = slice((2 + 2 * n_l) * c, (3 + 2 * n_l) * c)
            ek = slice((3 + 2 * n_l) * c, (4 + 2 * n_l) * c)
            a_l = [_dot_nt((q[hd] * jnp.exp(e[hd][eq])).astype(BF16), (k[hd] * jnp.exp(e[hd][ek])).astype(BF16))
                   for hd in heads]
            att = [jnp.where(lev == n_l, a_l[hd], att[hd]) for hd in heads]

        st = [st_ref[hd] for hd in heads]
        o = [_dot_nt((q[hd] * jnp.exp(e[hd][0:c])).astype(BF16), st[hd].astype(BF16))
             + _dot(att[hd].astype(BF16), vb[hd]) for hd in heads]
        kl = [(k[hd] * jnp.exp(e[hd][c:2 * c])).astype(BF16) for hd in heads]
        upd = [lax.dot_general(vb[hd], kl[hd], (((0,), (0,)), ((), ())), preferred_element_type=F32)
               for hd in heads]
        for hd in heads:
            st_ref[hd] = st[hd] * jnp.exp(e[hd][c - 1:c, :]) + upd[hd]
            on = o[hd] * lax.rsqrt(jnp.mean(o[hd] * o[hd], axis=-1, keepdims=True) + cfg.eps) * g_ref[...]
            o_ref[0, rows, vcols[hd]] = on
        return carry

    lax.fori_loop(0, n_sub, chunk, 0, unroll=2 if n_sub % 2 == 0 else 1)

    @pl.when(step_i == pl.num_programs(1) - 1)
    def _():
        for hd in range(cfg.gla_heads):
            s_out_ref[0, hd] = st_ref[hd].T


def _gla(cfg, gq, gk, gv, la, s0, norm_g, lb):
    b, l, _ = gq.shape
    c, dk, dv, h = cfg.chunk, cfg.gla_dk, cfg.gla_dv, cfg.gla_heads
    assert l % lb == 0 and lb % c == 0
    stack, lev, n_levels = _gla_constants(c)
    mat = jnp.asarray(stack, BF16)
    lev = jnp.asarray(lev)
    seq = lambda w: pl.BlockSpec((1, lb, h * w), lambda bi, si: (bi, si, 0))
    state = pl.BlockSpec((1, h, dk, dv), lambda bi, si: (bi, 0, 0, 0))
    const = lambda a: pl.BlockSpec(a.shape, lambda bi, si: (0,) * a.ndim)
    return pl.pallas_call(
        functools.partial(_gla_kernel, cfg, n_levels, lb // c),
        grid=(b, l // lb),
        in_specs=[seq(dk), seq(dk), seq(dv), seq(dk), state, const(mat), const(lev), const(norm_g)],
        out_specs=[seq(dv), state],
        out_shape=[jax.ShapeDtypeStruct((b, l, h * dv), F32), jax.ShapeDtypeStruct((b, h, dk, dv), F32)],
        scratch_shapes=[pltpu.VMEM((h, dv, dk), F32)],
        compiler_params=_params(2, arbitrary_last=True),
        name="gla",
    )(gq, gk, gv, la, s0, mat, lev, norm_g)


def _merge_kernel(cfg, x_ref, oa_ref, ob_ref, gr_ref, ga_ref, gb_ref, wa_ref, wb_ref, wo_ref, g2_ref,
                  h_ref, hn_ref):
    a = _dot(oa_ref[...], wa_ref[...])
    bb = _dot((ob_ref[...] * gr_ref[...]).astype(BF16), wb_ref[...])
    mix = ga_ref[...] * a + gb_ref[...] * bb
    h = x_ref[...] + _dot(mix.astype(BF16), wo_ref[...])
    h_ref[...] = h
    hn = h * lax.rsqrt(jnp.mean(h * h, axis=-1, keepdims=True) + cfg.eps) * g2_ref[...]
    hn_ref[...] = hn.astype(BF16)


def _merge_out(cfg, x, oa, ob, gr, ga, gb, p, tm):
    t, d = x.shape
    tok = lambda a: pl.BlockSpec((tm, a.shape[1]), lambda i: (i, 0))
    const = lambda a: pl.BlockSpec(a.shape, lambda i: (0,) * a.ndim, pipeline_mode=pl.Buffered(1))
    toks = [x, oa, ob, gr, ga, gb]
    consts = [p["w_a"], p["w_b"], p["w_o"], p["g2"]]
    return pl.pallas_call(
        functools.partial(_merge_kernel, cfg),
        grid=(t // tm,),
        in_specs=[tok(a) for a in toks] + [const(a) for a in consts],
        out_specs=[pl.BlockSpec((tm, d), lambda i: (i, 0))] * 2,
        out_shape=[jax.ShapeDtypeStruct((t, d), F32), jax.ShapeDtypeStruct((t, d), BF16)],
        compiler_params=_params(1),
        name="merge_out",
    )(*toks, *consts)


def _top_values(x, k, with_rank=False):
    vals = []
    rank = jnp.full(x.shape, float(k), F32) if with_rank else None
    for n in range(k):
        m = jnp.max(x, axis=0, keepdims=True)
        vals.append(m)
        hit = x == m
        if with_rank:
            rank = jnp.where(hit, float(n), rank)
        x = jnp.where(hit, -jnp.inf, x)
    return (vals, rank) if with_rank else vals


def _rows_to_block(rows):
    k, t = len(rows), rows[0].shape[1]
    idx = lax.broadcasted_iota(jnp.int32, (k, t), 0)
    out = jnp.zeros((k, t), F32)
    for n, r in enumerate(rows):
        out = jnp.where(idx == n, r, out)
    return out


def _candidate_sums(top0, top1, k):
    tile = 8
    assert k % tile == 0 and k // (tile + 1) == 1
    blk0, blk1 = _rows_to_block(top0), _rows_to_block(top1)
    row = lax.broadcasted_iota(jnp.int32, (tile, top0[0].shape[1]), 0)
    pieces = []
    for b in range(tile):
        n_a = k // (b + 1)
        for a0 in range(0, n_a, tile):
            piece = blk0[a0:a0 + tile] + top1[b]
            pieces.append(piece if a0 + tile <= n_a else jnp.where(row < n_a - a0, piece, -jnp.inf))
    for b0 in range(tile, k, tile):
        pieces.append(blk1[b0:b0 + tile] + top0[0])
    return jnp.concatenate(pieces, axis=0)


def _peer_route_kernel(cfg, hn_ref, wq_ref, keys_ref, c1_ref, r2_ref, e1_ref, e2_ref):
    nk, topk = cfg.peer_n_keys, cfg.peer_topk
    sub = cfg.peer_qdim // 2
    q = _dot(hn_ref[...], wq_ref[...]).astype(BF16)
    for h in range(cfg.peer_heads):
        sc = [_dot_nt(keys_ref[2 * h + half], q[:, (2 * h + half) * sub:(2 * h + half + 1) * sub])
              for half in range(2)]
        top0 = _top_values(sc[0], topk)
        top1, rank1 = _top_values(sc[1], topk, with_rank=True)
        top = [top0, top1]
        best = _top_values(_candidate_sums(top0, top1, topk), topk)
        tau = best[topk - 1]
        z = best[0] * 0.0
        for cv in best:
            z = z + jnp.exp(cv - best[0])
        count = jnp.zeros(sc[0].shape, F32)
        for b in range(topk):
            count = count + jnp.where(sc[0] + top[1][b] >= tau, 1.0, 0.0)
        rows = pl.ds(h * nk, nk)
        c1_ref[rows, :] = count
        r2_ref[rows, :] = rank1
        e1_ref[rows, :] = jnp.exp(sc[0] - top[0][0]) / z
        e2_ref[rows, :] = jnp.exp(sc[1] - top[1][0])


def _peer_route(cfg, hn, wq, keys, tb):
    t, d = hn.shape
    hk = cfg.peer_heads * cfg.peer_n_keys
    const = lambda a: pl.BlockSpec(a.shape, lambda i: (0,) * a.ndim, pipeline_mode=pl.Buffered(1))
    col = lambda r: pl.BlockSpec((r, tb), lambda i: (0, i))
    return pl.pallas_call(
        functools.partial(_peer_route_kernel, cfg),
        grid=(t // tb,),
        in_specs=[pl.BlockSpec((tb, d), lambda i: (i, 0)), const(wq), const(keys)],
        out_specs=[col(hk)] * 4,
        out_shape=[jax.ShapeDtypeStruct((hk, t), F32)] * 4,
        compiler_params=_params(1),
        name="peer_route",
    )(hn, wq, keys)


def _gelu(x):
    return 0.5 * x * (1.0 + lax.erf(x * (2.0 ** -0.5)))


def _peer_dense_kernel(cfg, ib, hn_ref, h_ref, u_ref, vt_ref, c1_ref, r2_ref, e1_ref, e2_ref,
                       y_ref, acc_ref, act_ref, p_ref):
    nk = cfg.peer_n_keys
    ei = pl.program_id(1)
    tb = hn_ref.shape[0]

    @pl.when(ei == 0)
    def _():
        acc_ref[...] = jnp.zeros(acc_ref.shape, F32)

    act_ref[...] = _gelu(_dot_nt(u_ref[...], hn_ref[...]))

    n_row_tiles = nk // GATE_ROWS

    def gate_tile(n, carry):
        r0 = pl.multiple_of((n % n_row_tiles) * GATE_ROWS, GATE_ROWS)
        lanes = pl.ds(pl.multiple_of((n // n_row_tiles) * LANES, LANES), LANES)
        for i0 in range(0, ib, GATE_FIRST):
            g = [None] * GATE_FIRST
            for h in range(cfg.peer_heads):
                rows = pl.ds(h * nk + r0, GATE_ROWS)
                r2, e2 = r2_ref[rows, lanes], e2_ref[rows, lanes]
                for k in range(GATE_FIRST):
                    keep = r2 < c1_ref[h, i0 + k:i0 + k + 1, lanes]
                    term = jnp.where(keep, e2, 0.0) * e1_ref[h, i0 + k:i0 + k + 1, lanes]
                    g[k] = term if h == 0 else g[k] + term
            for k in range(GATE_FIRST):
                rows = pl.ds((i0 + k) * nk + r0, GATE_ROWS)
                p_ref[rows, lanes] = (g[k] * act_ref[rows, lanes]).astype(BF16)
        return carry

    lax.fori_loop(0, n_row_tiles * (tb // LANES), gate_tile, 0)
    acc_ref[...] += lax.dot_general(vt_ref[...], p_ref[...], (((0,), (0,)), ((), ())),
                                    preferred_element_type=F32)

    @pl.when(ei == pl.num_programs(1) - 1)
    def _():
        y_ref[...] = h_ref[...] + acc_ref[...].T


def _peer_dense(cfg, hn, h, u, vt, route, tb, ib):
    t, d = hn.shape
    nk = cfg.peer_n_keys
    eb = ib * nk
    hk = cfg.peer_heads * nk
    c1, r2, e1, e2 = route
    by_head = lambda a: a.reshape(cfg.peer_heads, nk, t)
    tokb = pl.BlockSpec((tb, d), lambda ti, ei: (ti, 0))
    col = lambda r: pl.BlockSpec((r, tb), lambda ti, ei: (0, ti))
    first = pl.BlockSpec((cfg.peer_heads, ib, tb), lambda ti, ei: (0, ei, ti))
    return pl.pallas_call(
        functools.partial(_peer_dense_kernel, cfg, ib),
        grid=(t // tb, cfg.n_experts // eb),
        in_specs=[tokb, tokb,
                  pl.BlockSpec((eb, d), lambda ti, ei: (ei, 0)),
                  pl.BlockSpec((eb, d), lambda ti, ei: (ei, 0)),
                  first, col(hk), first, col(hk)],
        out_specs=tokb,
        out_shape=jax.ShapeDtypeStruct((t, d), F32),
        scratch_shapes=[pltpu.VMEM((d, tb), F32), pltpu.VMEM((eb, tb), F32), pltpu.VMEM((eb, tb), BF16)],
        compiler_params=_params(2, arbitrary_last=True),
        name="peer_dense",
    )(hn, h, u, vt, by_head(c1), r2, by_head(e1), e2)


def _prep_params(cfg, norm1_g, w_in, qk_norm_g, da_lambda, da_subln_g, gla_w_gate2, gla_b_gate, gla_norm_g,
                 w_branch_a, w_branch_b, w_out, norm2_g, peer_w_q, peer_sub_keys, peer_u, peer_v):
    d, qw, gqk, gvw, rank = cfg.d_model, cfg.da_w, cfg.gla_qk_w, cfg.gla_v_w, cfg.gla_gate_rank
    splits = (qw, qw, qw, gqk, gqk, gvw, rank, gvw, d, d)
    starts = np.concatenate([[0], np.cumsum(splits)])
    piece = lambda n: w_in[:, starts[n]:starts[n + 1]]
    glr = jnp.pad(piece(6), ((0, 0), (0, LANES - rank)))
    w_all = jnp.concatenate([piece(n) for n in (0, 1, 2, 3, 4, 5, 7, 8, 9)] + [glr], axis=1).astype(BF16)
    group = np.arange(qw) // cfg.da_head_dim
    seg = (group[:, None] == np.arange(LANES)[None, :]).astype(np.float32)
    return {
        "g1": norm1_g[None, :], "w_all": w_all,
        "qk_gain": jnp.tile(qk_norm_g, (1, qw // cfg.da_head_dim)),
        "seg": jnp.asarray(seg, BF16),
        "w_gate2": jnp.pad(gla_w_gate2, ((0, LANES - rank), (0, 0))).astype(BF16),
        "b_gate": gla_b_gate[None, :],
        "da_lambda": da_lambda, "subln_g": da_subln_g[None, :], "gla_norm_g": gla_norm_g[None, :],
        "w_a": w_branch_a.astype(BF16), "w_b": w_branch_b.astype(BF16), "w_o": w_out.astype(BF16),
        "g2": norm2_g[None, :],
        "wq": peer_w_q.astype(BF16),
        "keys": peer_sub_keys.reshape(cfg.peer_heads * 2, cfg.peer_n_keys, cfg.peer_qdim // 2).astype(BF16),
        "u": peer_u.astype(BF16), "vt": peer_v.astype(BF16),
    }


def _pick(n, target):
    t = min(n, target)
    while n % t:
        t //= 2
    return t


def _layer(cfg, x, cache_k, cache_v, s0, p, lam_init):
    b, l, d = x.shape
    t = b * l
    past = 0 if cache_k is None else cache_k.shape[1]
    qw = cfg.da_w

    tm = _pick(t, 256)
    if l % tm == 0:
        positions = past + jnp.arange(l)
    else:
        positions = past + (jnp.arange(tm) % l)
    cos, sin = _rope_tables(cfg, positions)
    q, k, kb, v, vb, gq, gk, gv, la, gr, ga, gb = _in_proj(cfg, x.reshape(t, d), cos, sin, p, tm)

    r3 = lambda a: a.reshape(b, l, a.shape[-1])
    if cache_k is None:
        kk, vv = r3(kb), r3(vb)
        tq = _pick(l, ATTN_TQ)
        tk = _pick(l, ATTN_TK)
    else:
        tq = _pick(l, ATTN_TQ)
        tk = -(-(past + l) // LANES) * LANES
        pad = jnp.zeros((b, tk - past - l, qw), BF16)
        kk = jnp.concatenate([cache_k.reshape(b, past, qw).astype(BF16), r3(kb), pad], axis=1)
        vv = jnp.concatenate([cache_v.reshape(b, past, qw).astype(BF16), r3(vb), pad], axis=1)
    oa = _diff_attn(cfg, r3(q), kk, vv, p["da_lambda"], p["subln_g"], past, tq, tk, lam_init)

    if s0 is None:
        s0 = jnp.zeros((b, cfg.gla_heads, cfg.gla_dk, cfg.gla_dv), F32)
    ob, s_new = _gla(cfg, r3(gq), r3(gk), r3(gv), r3(la), s0, p["gla_norm_g"], _pick(l, 512))

    h, hn = _merge_out(cfg, x.reshape(t, d), oa.reshape(t, qw), ob.reshape(t, -1), gr, ga, gb, p, tm)

    tb = _pick(t, 512)
    route = _peer_route(cfg, hn, p["wq"], p["keys"], _pick(t, ROUTE_TOKENS))
    y = _peer_dense(cfg, hn, h, p["u"], p["vt"], route, tb, PEER_FIRST_PER_STEP)

    k_new = k.reshape(b, l, cfg.da_heads, 2, cfg.da_head_dim)
    v_new = v.reshape(b, l, cfg.da_heads, cfg.da_v_dim)
    return y.reshape(b, l, d), k_new, v_new, s_new


def _forward(cfg, x_prompt, x_sample, cache_k, cache_v, state_gla, *weights):
    yp, ys = x_prompt, x_sample
    outs = [[] for _ in range(6)]
    for layer in range(cache_k.shape[0]):
        p = _prep_params(cfg, *[w[layer] for w in weights])
        lam_init = 0.8 - 0.6 * math.exp(-0.3 * layer)
        yp, kp, vp, sp = _layer(cfg, yp, None, None, None, p, lam_init)
        ys, kn, vn, sn = _layer(cfg, ys, cache_k[layer], cache_v[layer], state_gla[layer], p, lam_init)
        for lst, a in zip(outs, (kp, vp, sp, kn, vn, sn)):
            lst.append(a)
    return (yp, ys) + tuple(jnp.stack(o) for o in outs)


def kernel(x_prompt, x_sample, cache_k, cache_v, state_gla, norm1_g, w_in, qk_norm_g, da_lambda, da_subln_g,
           gla_w_gate2, gla_b_gate, gla_norm_g, w_branch_a, w_branch_b, w_out, norm2_g, peer_w_q,
           peer_sub_keys, peer_u, peer_v):
    return _forward(Cfg(), x_prompt, x_sample, cache_k, cache_v, state_gla, norm1_g, w_in, qk_norm_g, da_lambda,
                    da_subln_g, gla_w_gate2, gla_b_gate, gla_norm_g, w_branch_a, w_branch_b, w_out, norm2_g,
                    peer_w_q, peer_sub_keys, peer_u, peer_v)
```

```python
import functools
import math
from typing import NamedTuple

import jax
import jax.numpy as jnp
import numpy as np
from jax import lax
from jax.experimental import pallas as pl
from jax.experimental.pallas import tpu as pltpu

F32 = jnp.float32
BF16 = jnp.bfloat16
LANES = 128
VMEM_LIMIT = 56 * 1024 * 1024
GATE_ROWS = 32
ROUTE_TOKENS = 256
PEER_FIRST_PER_STEP = 16
GATE_FIRST = 4
ATTN_TQ = 512
ATTN_TK = 512
ATTN_MAX_STATIC_Q_BLOCKS = 1


class Cfg(NamedTuple):
    d_model: int = 1024
    chunk: int = 64
    eps: float = 1e-6
    da_heads: int = 8
    da_head_dim: int = 64
    rope_theta: float = 10000.0
    gla_heads: int = 4
    gla_gate_rank: int = 16
    gla_gate_tau: float = 16.0
    peer_heads: int = 8
    peer_n_keys: int = 128
    peer_qdim: int = 256
    peer_topk: int = 16

    @property
    def da_v_dim(self): return 2 * self.da_head_dim
    @property
    def da_w(self): return self.da_heads * 2 * self.da_head_dim
    @property
    def gla_dk(self): return self.d_model // 2 // self.gla_heads
    @property
    def gla_dv(self): return self.d_model // self.gla_heads
    @property
    def gla_qk_w(self): return self.gla_heads * self.gla_dk
    @property
    def gla_v_w(self): return self.gla_heads * self.gla_dv
    @property
    def n_experts(self): return self.peer_n_keys * self.peer_n_keys


def _dot(a, b):
    return jnp.dot(a, b, preferred_element_type=F32)


def _dot_nt(a, b):
    return lax.dot_general(a, b, (((1,), (1,)), ((), ())), preferred_element_type=F32)


def _split_bf16(t):
    hi = t.astype(BF16)
    lo = (t - hi.astype(F32)).astype(BF16)
    return hi, lo


def _params(n_axes, arbitrary_last=False):
    sem = ["parallel"] * n_axes
    if arbitrary_last:
        sem[-1] = "arbitrary"
    return pltpu.CompilerParams(dimension_semantics=tuple(sem), vmem_limit_bytes=VMEM_LIMIT)


def _rope_table_kernel(pos_ref, freq_ref, sign_ref, cos_ref, sin_ref):
    ang = pos_ref[...] * freq_ref[...]
    cos_ref[...] = jnp.cos(ang)
    sin_ref[...] = jnp.sin(ang) * sign_ref[...]


def _rope_tables(cfg, positions):
    half = cfg.da_head_dim // 2
    r = positions.shape[0]
    lane = np.arange(LANES)
    inv_freq = cfg.rope_theta ** (-jnp.arange(half, dtype=F32) / half)
    freq = jnp.tile(inv_freq, LANES // half)[None, :]
    sign = jnp.asarray(np.where(lane % cfg.da_head_dim < half, -1.0, 1.0), F32)[None, :]
    pos = jnp.broadcast_to(positions.astype(F32)[:, None], (r, LANES))
    full = lambda shape: pl.BlockSpec(shape, lambda: (0, 0))
    return pl.pallas_call(
        _rope_table_kernel,
        out_shape=(jax.ShapeDtypeStruct((r, LANES), F32),) * 2,
        in_specs=[full((r, LANES)), full((1, LANES)), full((1, LANES))],
        out_specs=(full((r, LANES)),) * 2,
        name="rope_table",
    )(pos, freq, sign)


def _log_sigmoid(x):
    return jnp.minimum(x, 0.0) - jnp.log1p(jnp.exp(-jnp.abs(x)))


def _in_proj_kernel(cfg, x_ref, g1_ref, w_ref, cos_ref, sin_ref, qkg_ref, seg_ref, w2_ref, b2_ref,
                    q_out, k_out, kb_out, v_out, vb_out, gq_out, gk_out, gv_out, la_out, gr_out, ga_out, gb_out):
    d = cfg.d_model
    qw, gqk, gvw = cfg.da_w, cfg.gla_qk_w, cfg.gla_v_w
    x = x_ref[...]
    xn = (x * lax.rsqrt(jnp.mean(x * x, axis=-1, keepdims=True) + cfg.eps) * g1_ref[...]).astype(BF16)

    widths = (("q", qw), ("k", qw), ("v", qw), ("gq", gqk), ("gk", gqk), ("gv", gvw), ("gr", gvw),
              ("ga", d), ("gb", d), ("glr", LANES))
    cols, start = {}, 0
    for name, width in widths:
        cols[name] = slice(start, start + width)
        start += width

    def proj(name):
        return _dot(xn, w_ref[:, cols[name]])

    nrep = qw // LANES
    cos = jnp.tile(cos_ref[...], (1, nrep))
    sin = jnp.tile(sin_ref[...], (1, nrep))
    lane = lax.broadcasted_iota(jnp.int32, (x.shape[0], qw), 1)
    first_half = (lane & (cfg.da_head_dim - 1)) < (cfg.da_head_dim // 2)
    seg = seg_ref[...]
    lane128 = lax.broadcasted_iota(jnp.int32, (x.shape[0], LANES), 1)

    def group_sums(t):
        hi, lo = _split_bf16(t * t)
        return _dot(hi, seg) + _dot(lo, seg)

    def qk_norm_rope(t, ss, gain):
        r = lax.rsqrt(ss * (1.0 / cfg.da_head_dim) + cfg.eps)
        per_tile = LANES // cfg.da_head_dim
        spread = []
        for v in range(nrep):
            tile = r[:, v * per_tile:v * per_tile + 1]
            for g in range(1, per_tile):
                col = v * per_tile + g
                tile = jnp.where(lane128 < g * cfg.da_head_dim, tile, r[:, col:col + 1])
            spread.append(jnp.broadcast_to(tile, (t.shape[0], LANES)))
        y = t * jnp.concatenate(spread, axis=1) * gain
        half = cfg.da_head_dim // 2
        partner = jnp.where(first_half, pltpu.roll(y, qw - half, 1), pltpu.roll(y, half, 1))
        return y * cos + partner * sin

    q_raw = proj("q")
    q = qk_norm_rope(q_raw, group_sums(q_raw), qkg_ref[0:1, :])
    q_out[...] = (q * (cfg.da_head_dim ** -0.5 * math.log2(math.e))).astype(BF16)
    k_raw = proj("k")
    k = qk_norm_rope(k_raw, group_sums(k_raw), qkg_ref[1:2, :])
    for hm in range(qw // cfg.da_head_dim):
        k_out[:, hm // 2, hm % 2, :] = k[:, hm * cfg.da_head_dim:(hm + 1) * cfg.da_head_dim]
    kb_out[...] = k.astype(BF16)
    v = proj("v")
    v_out[...] = v
    vb_out[...] = v.astype(BF16)
    gq_out[...] = proj("gq") * (cfg.gla_dk ** -0.5)
    gk_out[...] = proj("gk")
    gv_out[...] = proj("gv")
    gr = proj("gr")
    gr_out[...] = gr * jax.nn.sigmoid(gr)
    ga_out[...] = jax.nn.sigmoid(proj("ga"))
    gb_out[...] = jax.nn.sigmoid(proj("gb"))
    gate = _dot(proj("glr").astype(BF16), w2_ref[...]) + b2_ref[...]
    la_out[...] = _log_sigmoid(gate) * (1.0 / cfg.gla_gate_tau)


def _in_proj(cfg, x, cos, sin, p, tm):
    t, d = x.shape
    qw, gqk, gvw = cfg.da_w, cfg.gla_qk_w, cfg.gla_v_w
    n_tab = cos.shape[0] // tm
    tok = lambda w: pl.BlockSpec((tm, w), lambda i: (i, 0))
    const = lambda a: pl.BlockSpec(a.shape, lambda i: (0,) * a.ndim, pipeline_mode=pl.Buffered(1))
    tab = pl.BlockSpec((tm, LANES), lambda i: (i % n_tab, 0))
    consts = [p["g1"], p["w_all"]]
    consts2 = [p["qk_gain"], p["seg"], p["w_gate2"], p["b_gate"]]
    outs = [(qw, BF16), (qw, F32), (qw, BF16), (qw, F32), (qw, BF16), (gqk, F32), (gqk, F32), (gvw, F32),
            (gqk, F32), (gvw, F32), (d, F32), (d, F32)]
    cache_shape = {1: (t, cfg.da_heads, 2, cfg.da_head_dim)}
    cache_spec = {n: pl.BlockSpec((tm,) + s[1:], lambda i, nd=len(s): (i,) + (0,) * (nd - 1))
                  for n, s in cache_shape.items()}
    return pl.pallas_call(
        functools.partial(_in_proj_kernel, cfg),
        grid=(t // tm,),
        in_specs=[tok(d)] + [const(a) for a in consts] + [tab, tab] + [const(a) for a in consts2],
        out_specs=[cache_spec.get(n, tok(w)) for n, (w, _) in enumerate(outs)],
        out_shape=[jax.ShapeDtypeStruct(cache_shape.get(n, (t, w)), dt) for n, (w, dt) in enumerate(outs)],
        compiler_params=_params(1),
        name="in_proj",
    )(x, *consts, cos, sin, *consts2)


def _diff_attn_kernel(cfg, past, tq, tk, n_q, heads_per_step, lam_init,
                      q_ref, k_ref, v_ref, lamp_ref, g_ref, o_ref, *scratch):
    qi = pl.program_id(2)
    hd, vd = cfg.da_head_dim, cfg.da_v_dim

    def blocks(q_start):
        n_blocks = (q_start + tq + tk - 1) // tk
        n_full = (q_start + cfg.chunk) // tk
        return n_full, n_blocks

    for hh in range(heads_per_step):
        cols = slice(hh * vd, (hh + 1) * vd)
        q = q_ref[0, :, cols]
        lane = lax.broadcasted_iota(jnp.int32, q.shape, 1)
        zero = jnp.zeros_like(q)
        qs = jnp.concatenate([jnp.where(lane < hd, q, zero), jnp.where(lane >= hd, q, zero)], axis=0)
        refs = (k_ref.at[:, :, cols], v_ref.at[:, :, cols], lamp_ref, g_ref, o_ref.at[:, :, cols]) + scratch
        attend = functools.partial(_attend, cfg, tq, tk, lam_init, refs, qs)
        if n_q <= ATTN_MAX_STATIC_Q_BLOCKS:
            for c in range(n_q):
                pl.when(qi == c)(functools.partial(attend, past + c * tq, *blocks(past + c * tq), True))
        else:
            q_start = past + qi * tq
            attend(q_start, *blocks(q_start), False)


def _attend(cfg, tq, tk, lam_init, refs, qs, q_start, n_full, n_blocks, static):
    k_ref, v_ref, lamp_ref, g_ref, o_ref, s_ref, mpart_ref, mrow_ref, lpart_ref, acc_ref = refs
    shift = int(math.log2(cfg.chunk))
    n_lane_chunks = tk // LANES
    mpart_ref[...] = jnp.full(mpart_ref.shape, -jnp.inf, F32)

    def scores(j, masked):
        ks = j * tk if static else pl.multiple_of(j * tk, tk)
        s = _dot_nt(qs, k_ref[0, pl.ds(ks, tk), :])
        if masked:
            row = lax.broadcasted_iota(jnp.int32, s.shape, 0)
            col = lax.broadcasted_iota(jnp.int32, s.shape, 1)
            qpos = q_start + jnp.where(row >= tq, row - tq, row)
            allowed = lax.shift_right_logical(ks + col, shift) <= lax.shift_right_logical(qpos, shift)
            s = jnp.where(allowed, s, -jnp.inf)
        s_ref[j] = s
        part = s[:, 0:LANES]
        for c in range(1, n_lane_chunks):
            part = jnp.maximum(part, s[:, c * LANES:(c + 1) * LANES])
        mpart_ref[...] = jnp.maximum(mpart_ref[...], part)

    def in_pairs(fn, lo, hi):
        def pair(n, carry):
            fn(lo + 2 * n)
            fn(lo + 2 * n + 1)
            return carry

        lax.fori_loop(0, (hi - lo) // 2, pair, 0)

        @pl.when((hi - lo) % 2 == 1)
        def _():
            fn(hi - 1)

    if static:
        for j in range(n_blocks):
            scores(j, j >= n_full)
    else:
        in_pairs(lambda j: scores(j, False), 0, n_full)
        in_pairs(lambda j: scores(j, True), n_full, n_blocks)

    mrow_ref[...] = jnp.broadcast_to(jnp.max(mpart_ref[...], axis=-1, keepdims=True), mrow_ref.shape)
    lpart_ref[...] = jnp.zeros(lpart_ref.shape, F32)
    acc_ref[...] = jnp.zeros(acc_ref.shape, F32)

    def weigh(j, carry):
        ks = j * tk if static else pl.multiple_of(j * tk, tk)
        m = mrow_ref[...]
        ps = [jnp.exp2(s_ref[j, :, c * LANES:(c + 1) * LANES] - m) for c in range(n_lane_chunks)]
        lsum = ps[0]
        for pc in ps[1:]:
            lsum = lsum + pc
        lpart_ref[...] += lsum
        p = jnp.concatenate([pc.astype(BF16) for pc in ps], axis=1)
        acc_ref[...] += _dot(p, v_ref[0, pl.ds(ks, tk), :])
        return carry

    if static:
        for j in range(n_blocks):
            weigh(j, 0)
    else:
        in_pairs(lambda j: weigh(j, 0), 0, n_blocks)

    lp = lamp_ref[...]
    lam = (jnp.exp(jnp.sum(lp[0:1] * lp[1:2], axis=-1, keepdims=True))
           - jnp.exp(jnp.sum(lp[2:3] * lp[3:4], axis=-1, keepdims=True)) + lam_init)
    o = acc_ref[...] / jnp.sum(lpart_ref[...], axis=-1, keepdims=True)
    o = o[:tq] - lam * o[tq:]
    o = o * lax.rsqrt(jnp.mean(o * o, axis=-1, keepdims=True) + cfg.eps) * g_ref[...] * (1.0 - lam_init)
    o_ref[0] = o.astype(BF16)


def _diff_attn(cfg, q, k, v, lam_p, subln_g, past, tq, tk, lam_init):
    b, lq, w = q.shape
    lk = k.shape[1]
    vd = cfg.da_v_dim
    assert lq % tq == 0 and lk % tk == 0 and tk % LANES == 0 and tq % cfg.chunk == 0 and past % cfg.chunk == 0
    assert past + lq <= lk
    hps = cfg.da_heads if lq // tq <= ATTN_MAX_STATIC_Q_BLOCKS else 1
    kv_spec = pl.BlockSpec((1, lk, hps * vd), lambda bi, h, qi: (bi, 0, h))
    q_spec = pl.BlockSpec((1, tq, hps * vd), lambda bi, h, qi: (bi, qi, h))
    return pl.pallas_call(
        functools.partial(_diff_attn_kernel, cfg, past, tq, tk, lq // tq, hps, lam_init),
        grid=(b, cfg.da_heads // hps, lq // tq),
        in_specs=[q_spec, kv_spec, kv_spec,
                  pl.BlockSpec(lam_p.shape, lambda bi, h, qi: (0, 0)),
                  pl.BlockSpec(subln_g.shape, lambda bi, h, qi: (0, 0))],
        out_specs=q_spec,
        out_shape=jax.ShapeDtypeStruct((b, lq, w), BF16),
        scratch_shapes=[pltpu.VMEM((lk // tk, 2 * tq, tk), F32), pltpu.VMEM((2 * tq, LANES), F32),
                        pltpu.VMEM((2 * tq, LANES), F32), pltpu.VMEM((2 * tq, LANES), F32),
                        pltpu.VMEM((2 * tq, vd), F32)],
        compiler_params=_params(3),
        name="diff_attn",
    )(q, k, v, lam_p, subln_g)


def _gla_constants(c):
    r = np.arange(c)
    i = r[:, None]
    rr = r[None, :]
    mats = [(rr <= i), (rr > i)]
    levels = []
    s = c // 2
    while s >= 1:
        levels.append(s)
        s //= 2
    for s in levels:
        bd = (i // s) * s
        mats.append((rr > bd) & (rr <= i))
        bd2 = (i // s + 1) * s
        mats.append((rr > i) & (rr <= bd2))
    stack = np.concatenate([m.astype(np.float32) for m in mats], axis=0)
    lev = np.full((c, c), -1, np.int32)
    jj = r[None, :]
    for n, s in enumerate(levels):
        sel = (jj < i) & (i // s != jj // s) & (lev < 0)
        lev[sel] = n
    lev[i == jj] = len(levels)
    return stack, lev, len(levels)


def _gla_kernel(cfg, n_levels, n_sub, q_ref, k_ref, v_ref, a_ref, s0_ref, mat_ref, lev_ref, g_ref,
                o_ref, s_out_ref, st_ref):
    c, dk, dv = cfg.chunk, cfg.gla_dk, cfg.gla_dv
    step_i = pl.program_id(1)

    @pl.when(step_i == 0)
    def _():
        for hd in range(cfg.gla_heads):
            st_ref[hd] = s0_ref[0, hd].T

    mat = mat_ref[...]
    lev = lev_ref[...]

    heads = range(cfg.gla_heads)
    kcols = [slice(hd * dk, (hd + 1) * dk) for hd in heads]
    vcols = [slice(hd * dv, (hd + 1) * dv) for hd in heads]

    def chunk(n, carry):
        rows = pl.ds(pl.multiple_of(n * c, c), c)
        a_hi, a_lo = _split_bf16(a_ref[0, rows, :])
        e_all = _dot(mat, a_hi) + _dot(mat, a_lo)
        e = [e_all[:, kcols[hd]] for hd in heads]
        q = [q_ref[0, rows, kcols[hd]] for hd in heads]
        k = [k_ref[0, rows, kcols[hd]] for hd in heads]
        vb = [v_ref[0, rows, vcols[hd]].astype(BF16) for hd in heads]

        att = [jnp.where(lev == n_levels, _dot_nt(q[hd].astype(BF16), k[hd].astype(BF16)), 0.0) for hd in heads]
        for n_l in range(n_levels):
            eq = slice((2 + 2 * n_l) * c, (3 + 2 * n_l) * c)
            ek = slice((3 + 2 * n_l) * c, (4 + 2 * n_l) * c)
            a_l = [_dot_nt((q[hd] * jnp.exp(e[hd][eq])).astype(BF16), (k[hd] * jnp.exp(e[hd][ek])).astype(BF16))
                   for hd in heads]
            att = [jnp.where(lev == n_l, a_l[hd], att[hd]) for hd in heads]

        st = [st_ref[hd] for hd in heads]
        o = [_dot_nt((q[hd] * jnp.exp(e[hd][0:c])).astype(BF16), st[hd].astype(BF16))
             + _dot(att[hd].astype(BF16), vb[hd]) for hd in heads]
        kl = [(k[hd] * jnp.exp(e[hd][c:2 * c])).astype(BF16) for hd in heads]
        upd = [lax.dot_general(vb[hd], kl[hd], (((0,), (0,)), ((), ())), preferred_element_type=F32)
               for hd in heads]
        for hd in heads:
            st_ref[hd] = st[hd] * jnp.exp(e[hd][c - 1:c, :]) + upd[hd]
            on = o[hd] * lax.rsqrt(jnp.mean(o[hd] * o[hd], axis=-1, keepdims=True) + cfg.eps) * g_ref[...]
            o_ref[0, rows, vcols[hd]] = on
        return carry

    lax.fori_loop(0, n_sub, chunk, 0, unroll=2 if n_sub % 2 == 0 else 1)

    @pl.when(step_i == pl.num_programs(1) - 1)
    def _():
        for hd in range(cfg.gla_heads):
            s_out_ref[0, hd] = st_ref[hd].T


def _gla(cfg, gq, gk, gv, la, s0, norm_g, lb):
    b, l, _ = gq.shape
    c, dk, dv, h = cfg.chunk, cfg.gla_dk, cfg.gla_dv, cfg.gla_heads
    assert l % lb == 0 and lb % c == 0
    stack, lev, n_levels = _gla_constants(c)
    mat = jnp.asarray(stack, BF16)
    lev = jnp.asarray(lev)
    seq = lambda w: pl.BlockSpec((1, lb, h * w), lambda bi, si: (bi, si, 0))
    state = pl.BlockSpec((1, h, dk, dv), lambda bi, si: (bi, 0, 0, 0))
    const = lambda a: pl.BlockSpec(a.shape, lambda bi, si: (0,) * a.ndim)
    return pl.pallas_call(
        functools.partial(_gla_kernel, cfg, n_levels, lb // c),
        grid=(b, l // lb),
        in_specs=[seq(dk), seq(dk), seq(dv), seq(dk), state, const(mat), const(lev), const(norm_g)],
        out_specs=[seq(dv), state],
        out_shape=[jax.ShapeDtypeStruct((b, l, h * dv), F32), jax.ShapeDtypeStruct((b, h, dk, dv), F32)],
        scratch_shapes=[pltpu.VMEM((h, dv, dk), F32)],
        compiler_params=_params(2, arbitrary_last=True),
        name="gla",
    )(gq, gk, gv, la, s0, mat, lev, norm_g)


def _merge_kernel(cfg, x_ref, oa_ref, ob_ref, gr_ref, ga_ref, gb_ref, wa_ref, wb_ref, wo_ref, g2_ref,
                  h_ref, hn_ref):
    a = _dot(oa_ref[...], wa_ref[...])
    bb = _dot((ob_ref[...] * gr_ref[...]).astype(BF16), wb_ref[...])
    mix = ga_ref[...] * a + gb_ref[...] * bb
    h = x_ref[...] + _dot(mix.astype(BF16), wo_ref[...])
    h_ref[...] = h
    hn = h * lax.rsqrt(jnp.mean(h * h, axis=-1, keepdims=True) + cfg.eps) * g2_ref[...]
    hn_ref[...] = hn.astype(BF16)


def _merge_out(cfg, x, oa, ob, gr, ga, gb, p, tm):
    t, d = x.shape
    tok = lambda a: pl.BlockSpec((tm, a.shape[1]), lambda i: (i, 0))
    const = lambda a: pl.BlockSpec(a.shape, lambda i: (0,) * a.ndim, pipeline_mode=pl.Buffered(1))
    toks = [x, oa, ob, gr, ga, gb]
    consts = [p["w_a"], p["w_b"], p["w_o"], p["g2"]]
    return pl.pallas_call(
        functools.partial(_merge_kernel, cfg),
        grid=(t // tm,),
        in_specs=[tok(a) for a in toks] + [const(a) for a in consts],
        out_specs=[pl.BlockSpec((tm, d), lambda i: (i, 0))] * 2,
        out_shape=[jax.ShapeDtypeStruct((t, d), F32), jax.ShapeDtypeStruct((t, d), BF16)],
        compiler_params=_params(1),
        name="merge_out",
    )(*toks, *consts)


def _top_values(x, k, with_rank=False):
    vals = []
    rank = jnp.full(x.shape, float(k), F32) if with_rank else None
    for n in range(k):
        m = jnp.max(x, axis=0, keepdims=True)
        vals.append(m)
        hit = x == m
        if with_rank:
            rank = jnp.where(hit, float(n), rank)
        x = jnp.where(hit, -jnp.inf, x)
    return (vals, rank) if with_rank else vals


def _rows_to_block(rows):
    k, t = len(rows), rows[0].shape[1]
    idx = lax.broadcasted_iota(jnp.int32, (k, t), 0)
    out = jnp.zeros((k, t), F32)
    for n, r in enumerate(rows):
        out = jnp.where(idx == n, r, out)
    return out


def _candidate_sums(top0, top1, k):
    tile = 8
    assert k % tile == 0 and k // (tile + 1) == 1
    blk0, blk1 = _rows_to_block(top0), _rows_to_block(top1)
    row = lax.broadcasted_iota(jnp.int32, (tile, top0[0].shape[1]), 0)
    pieces = []
    for b in range(tile):
        n_a = k // (b + 1)
        for a0 in range(0, n_a, tile):
            piece = blk0[a0:a0 + tile] + top1[b]
            pieces.append(piece if a0 + tile <= n_a else jnp.where(row < n_a - a0, piece, -jnp.inf))
    for b0 in range(tile, k, tile):
        pieces.append(blk1[b0:b0 + tile] + top0[0])
    return jnp.concatenate(pieces, axis=0)


def _peer_route_kernel(cfg, hn_ref, wq_ref, keys_ref, c1_ref, r2_ref, e1_ref, e2_ref):
    nk, topk = cfg.peer_n_keys, cfg.peer_topk
    sub = cfg.peer_qdim // 2
    q = _dot(hn_ref[...], wq_ref[...]).astype(BF16)
    for h in range(cfg.peer_heads):
        sc = [_dot_nt(keys_ref[2 * h + half], q[:, (2 * h + half) * sub:(2 * h + half + 1) * sub])
              for half in range(2)]
        top0 = _top_values(sc[0], topk)
        top1, rank1 = _top_values(sc[1], topk, with_rank=True)
        top = [top0, top1]
        best = _top_values(_candidate_sums(top0, top1, topk), topk)
        tau = best[topk - 1]
        z = best[0] * 0.0
        for cv in best:
            z = z + jnp.exp(cv - best[0])
        count = jnp.zeros(sc[0].shape, F32)
        for b in range(topk):
            count = count + jnp.where(sc[0] + top[1][b] >= tau, 1.0, 0.0)
        rows = pl.ds(h * nk, nk)
        c1_ref[rows, :] = count
        r2_ref[rows, :] = rank1
        e1_ref[rows, :] = jnp.exp(sc[0] - top[0][0]) / z
        e2_ref[rows, :] = jnp.exp(sc[1] - top[1][0])


def _peer_route(cfg, hn, wq, keys, tb):
    t, d = hn.shape
    hk = cfg.peer_heads * cfg.peer_n_keys
    const = lambda a: pl.BlockSpec(a.shape, lambda i: (0,) * a.ndim, pipeline_mode=pl.Buffered(1))
    col = lambda r: pl.BlockSpec((r, tb), lambda i: (0, i))
    return pl.pallas_call(
        functools.partial(_peer_route_kernel, cfg),
        grid=(t // tb,),
        in_specs=[pl.BlockSpec((tb, d), lambda i: (i, 0)), const(wq), const(keys)],
        out_specs=[col(hk)] * 4,
        out_shape=[jax.ShapeDtypeStruct((hk, t), F32)] * 4,
        compiler_params=_params(1),
        name="peer_route",
    )(hn, wq, keys)


def _gelu(x):
    return 0.5 * x * (1.0 + lax.erf(x * (2.0 ** -0.5)))


def _peer_dense_kernel(cfg, ib, hn_ref, h_ref, u_ref, vt_ref, c1_ref, r2_ref, e1_ref, e2_ref,
                       y_ref, acc_ref, act_ref, p_ref):
    nk = cfg.peer_n_keys
    ei = pl.program_id(1)
    tb = hn_ref.shape[0]

    @pl.when(ei == 0)
    def _():
        acc_ref[...] = jnp.zeros(acc_ref.shape, F32)

    act_ref[...] = _gelu(_dot_nt(u_ref[...], hn_ref[...]))

    n_row_tiles = nk // GATE_ROWS

    def gate_tile(n, carry):
        r0 = pl.multiple_of((n % n_row_tiles) * GATE_ROWS, GATE_ROWS)
        lanes = pl.ds(pl.multiple_of((n // n_row_tiles) * LANES, LANES), LANES)
        for i0 in range(0, ib, GATE_FIRST):
            g = [None] * GATE_FIRST
            for h in range(cfg.peer_heads):
                rows = pl.ds(h * nk + r0, GATE_ROWS)
                r2, e2 = r2_ref[rows, lanes], e2_ref[rows, lanes]
                for k in range(GATE_FIRST):
                    keep = r2 < c1_ref[h, i0 + k:i0 + k + 1, lanes]
                    term = jnp.where(keep, e2, 0.0) * e1_ref[h, i0 + k:i0 + k + 1, lanes]
                    g[k] = term if h == 0 else g[k] + term
            for k in range(GATE_FIRST):
                rows = pl.ds((i0 + k) * nk + r0, GATE_ROWS)
                p_ref[rows, lanes] = (g[k] * act_ref[rows, lanes]).astype(BF16)
        return carry

    n_tiles = n_row_tiles * (tb // LANES)
    lax.fori_loop(0, n_tiles, gate_tile, 0, unroll=2 if n_tiles % 2 == 0 else 1)
    acc_ref[...] += lax.dot_general(vt_ref[...], p_ref[...], (((0,), (0,)), ((), ())),
                                    preferred_element_type=F32)

    @pl.when(ei == pl.num_programs(1) - 1)
    def _():
        y_ref[...] = h_ref[...] + acc_ref[...].T


def _peer_dense(cfg, hn, h, u, vt, route, tb, ib):
    t, d = hn.shape
    nk = cfg.peer_n_keys
    eb = ib * nk
    hk = cfg.peer_heads * nk
    c1, r2, e1, e2 = route
    by_head = lambda a: a.reshape(cfg.peer_heads, nk, t)
    tokb = pl.BlockSpec((tb, d), lambda ti, ei: (ti, 0))
    col = lambda r: pl.BlockSpec((r, tb), lambda ti, ei: (0, ti))
    first = pl.BlockSpec((cfg.peer_heads, ib, tb), lambda ti, ei: (0, ei, ti))
    return pl.pallas_call(
        functools.partial(_peer_dense_kernel, cfg, ib),
        grid=(t // tb, cfg.n_experts // eb),
        in_specs=[tokb, tokb,
                  pl.BlockSpec((eb, d), lambda ti, ei: (ei, 0)),
                  pl.BlockSpec((eb, d), lambda ti, ei: (ei, 0)),
                  first, col(hk), first, col(hk)],
        out_specs=tokb,
        out_shape=jax.ShapeDtypeStruct((t, d), F32),
        scratch_shapes=[pltpu.VMEM((d, tb), F32), pltpu.VMEM((eb, tb), F32), pltpu.VMEM((eb, tb), BF16)],
        compiler_params=_params(2, arbitrary_last=True),
        name="peer_dense",
    )(hn, h, u, vt, by_head(c1), r2, by_head(e1), e2)


def _prep_params(cfg, norm1_g, w_in, qk_norm_g, da_lambda, da_subln_g, gla_w_gate2, gla_b_gate, gla_norm_g,
                 w_branch_a, w_branch_b, w_out, norm2_g, peer_w_q, peer_sub_keys, peer_u, peer_v):
    d, qw, gqk, gvw, rank = cfg.d_model, cfg.da_w, cfg.gla_qk_w, cfg.gla_v_w, cfg.gla_gate_rank
    splits = (qw, qw, qw, gqk, gqk, gvw, rank, gvw, d, d)
    starts = np.concatenate([[0], np.cumsum(splits)])
    piece = lambda n: w_in[:, starts[n]:starts[n + 1]]
    glr = jnp.pad(piece(6), ((0, 0), (0, LANES - rank)))
    w_all = jnp.concatenate([piece(n) for n in (0, 1, 2, 3, 4, 5, 7, 8, 9)] + [glr], axis=1).astype(BF16)
    group = np.arange(qw) // cfg.da_head_dim
    seg = (group[:, None] == np.arange(LANES)[None, :]).astype(np.float32)
    return {
        "g1": norm1_g[None, :], "w_all": w_all,
        "qk_gain": jnp.tile(qk_norm_g, (1, qw // cfg.da_head_dim)),
        "seg": jnp.asarray(seg, BF16),
        "w_gate2": jnp.pad(gla_w_gate2, ((0, LANES - rank), (0, 0))).astype(BF16),
        "b_gate": gla_b_gate[None, :],
        "da_lambda": da_lambda, "subln_g": da_subln_g[None, :], "gla_norm_g": gla_norm_g[None, :],
        "w_a": w_branch_a.astype(BF16), "w_b": w_branch_b.astype(BF16), "w_o": w_out.astype(BF16),
        "g2": norm2_g[None, :],
        "wq": peer_w_q.astype(BF16),
        "keys": peer_sub_keys.reshape(cfg.peer_heads * 2, cfg.peer_n_keys, cfg.peer_qdim // 2).astype(BF16),
        "u": peer_u.astype(BF16), "vt": peer_v.astype(BF16),
    }


def _pick(n, target):
    t = min(n, target)
    while n % t:
        t //= 2
    return t


def _layer(cfg, x, cache_k, cache_v, s0, p, lam_init):
    b, l, d = x.shape
    t = b * l
    past = 0 if cache_k is None else cache_k.shape[1]
    qw = cfg.da_w

    tm = _pick(t, 256)
    if l % tm == 0:
        positions = past + jnp.arange(l)
    else:
        positions = past + (jnp.arange(tm) % l)
    cos, sin = _rope_tables(cfg, positions)
    q, k, kb, v, vb, gq, gk, gv, la, gr, ga, gb = _in_proj(cfg, x.reshape(t, d), cos, sin, p, tm)

    r3 = lambda a: a.reshape(b, l, a.shape[-1])
    if cache_k is None:
        kk, vv = r3(kb), r3(vb)
        tq = _pick(l, ATTN_TQ)
        tk = _pick(l, ATTN_TK)
    else:
        tq = _pick(l, ATTN_TQ)
        tk = -(-(past + l) // LANES) * LANES
        pad = jnp.zeros((b, tk - past - l, qw), BF16)
        kk = jnp.concatenate([cache_k.reshape(b, past, qw).astype(BF16), r3(kb), pad], axis=1)
        vv = jnp.concatenate([cache_v.reshape(b, past, qw).astype(BF16), r3(vb), pad], axis=1)
    oa = _diff_attn(cfg, r3(q), kk, vv, p["da_lambda"], p["subln_g"], past, tq, tk, lam_init)

    if s0 is None:
        s0 = jnp.zeros((b, cfg.gla_heads, cfg.gla_dk, cfg.gla_dv), F32)
    ob, s_new = _gla(cfg, r3(gq), r3(gk), r3(gv), r3(la), s0, p["gla_norm_g"], _pick(l, 512))

    h, hn = _merge_out(cfg, x.reshape(t, d), oa.reshape(t, qw), ob.reshape(t, -1), gr, ga, gb, p, tm)

    tb = _pick(t, 512)
    route = _peer_route(cfg, hn, p["wq"], p["keys"], _pick(t, ROUTE_TOKENS))
    y = _peer_dense(cfg, hn, h, p["u"], p["vt"], route, tb, PEER_FIRST_PER_STEP)

    k_new = k.reshape(b, l, cfg.da_heads, 2, cfg.da_head_dim)
    v_new = v.reshape(b, l, cfg.da_heads, cfg.da_v_dim)
    return y.reshape(b, l, d), k_new, v_new, s_new


def _forward(cfg, x_prompt, x_sample, cache_k, cache_v, state_gla, *weights):
    yp, ys = x_prompt, x_sample
    outs = [[] for _ in range(6)]
    for layer in range(cache_k.shape[0]):
        p = _prep_params(cfg, *[w[layer] for w in weights])
        lam_init = 0.8 - 0.6 * math.exp(-0.3 * layer)
        yp, kp, vp, sp = _layer(cfg, yp, None, None, None, p, lam_init)
        ys, kn, vn, sn = _layer(cfg, ys, cache_k[layer], cache_v[layer], state_gla[layer], p, lam_init)
        for lst, a in zip(outs, (kp, vp, sp, kn, vn, sn)):
            lst.append(a)
    return (yp, ys) + tuple(jnp.stack(o) for o in outs)


def kernel(x_prompt, x_sample, cache_k, cache_v, state_gla, norm1_g, w_in, qk_norm_g, da_lambda, da_subln_g,
           gla_w_gate2, gla_b_gate, gla_norm_g, w_branch_a, w_branch_b, w_out, norm2_g, peer_w_q,
           peer_sub_keys, peer_u, peer_v):
    return _forward(Cfg(), x_prompt, x_sample, cache_k, cache_v, state_gla, norm1_g, w_in, qk_norm_g, da_lambda,
                    da_subln_g, gla_w_gate2, gla_b_gate, gla_norm_g, w_branch_a, w_branch_b, w_out, norm2_g,
                    peer_w_q, peer_sub_keys, peer_u, peer_v)
```

```python
import functools
import math
from typing import NamedTuple

import jax
import jax.numpy as jnp
import numpy as np
from jax import lax
from jax.experimental import pallas as pl
from jax.experimental.pallas import tpu as pltpu

F32 = jnp.float32
BF16 = jnp.bfloat16
LANES = 128
VMEM_LIMIT = 56 * 1024 * 1024
GATE_ROWS = 32
ROUTE_TOKENS = 256
PEER_FIRST_PER_STEP = 16
GATE_FIRST = 4
ATTN_TQ = 512
ATTN_TK = 512
ATTN_MAX_STATIC_Q_BLOCKS = 1


class Cfg(NamedTuple):
    d_model: int = 1024
    chunk: int = 64
    eps: float = 1e-6
    da_heads: int = 8
    da_head_dim: int = 64
    rope_theta: float = 10000.0
    gla_heads: int = 4
    gla_gate_rank: int = 16
    gla_gate_tau: float = 16.0
    peer_heads: int = 8
    peer_n_keys: int = 128
    peer_qdim: int = 256
    peer_topk: int = 16

    @property
    def da_v_dim(self): return 2 * self.da_head_dim
    @property
    def da_w(self): return self.da_heads * 2 * self.da_head_dim
    @property
    def gla_dk(self): return self.d_model // 2 // self.gla_heads
    @property
    def gla_dv(self): return self.d_model // self.gla_heads
    @property
    def gla_qk_w(self): return self.gla_heads * self.gla_dk
    @property
    def gla_v_w(self): return self.gla_heads * self.gla_dv
    @property
    def n_experts(self): return self.peer_n_keys * self.peer_n_keys


def _dot(a, b):
    return jnp.dot(a, b, preferred_element_type=F32)


def _dot_nt(a, b):
    return lax.dot_general(a, b, (((1,), (1,)), ((), ())), preferred_element_type=F32)


def _split_bf16(t):
    hi = t.astype(BF16)
    lo = (t - hi.astype(F32)).astype(BF16)
    return hi, lo


def _params(n_axes, arbitrary_last=False):
    sem = ["parallel"] * n_axes
    if arbitrary_last:
        sem[-1] = "arbitrary"
    return pltpu.CompilerParams(dimension_semantics=tuple(sem), vmem_limit_bytes=VMEM_LIMIT)


def _rope_table_kernel(pos_ref, freq_ref, sign_ref, cos_ref, sin_ref):
    ang = pos_ref[...] * freq_ref[...]
    cos_ref[...] = jnp.cos(ang)
    sin_ref[...] = jnp.sin(ang) * sign_ref[...]


def _rope_tables(cfg, positions):
    half = cfg.da_head_dim // 2
    r = positions.shape[0]
    lane = np.arange(LANES)
    inv_freq = cfg.rope_theta ** (-jnp.arange(half, dtype=F32) / half)
    freq = jnp.tile(inv_freq, LANES // half)[None, :]
    sign = jnp.asarray(np.where(lane % cfg.da_head_dim < half, -1.0, 1.0), F32)[None, :]
    pos = jnp.broadcast_to(positions.astype(F32)[:, None], (r, LANES))
    full = lambda shape: pl.BlockSpec(shape, lambda: (0, 0))
    return pl.pallas_call(
        _rope_table_kernel,
        out_shape=(jax.ShapeDtypeStruct((r, LANES), F32),) * 2,
        in_specs=[full((r, LANES)), full((1, LANES)), full((1, LANES))],
        out_specs=(full((r, LANES)),) * 2,
        name="rope_table",
    )(pos, freq, sign)


def _log_sigmoid(x):
    return jnp.minimum(x, 0.0) - jnp.log1p(jnp.exp(-jnp.abs(x)))


def _in_proj_kernel(cfg, x_ref, g1_ref, w_ref, cos_ref, sin_ref, qkg_ref, seg_ref, w2_ref, b2_ref,
                    q_out, k_out, kb_out, v_out, vb_out, gq_out, gk_out, gv_out, la_out, gr_out, ga_out, gb_out):
    d = cfg.d_model
    qw, gqk, gvw = cfg.da_w, cfg.gla_qk_w, cfg.gla_v_w
    x = x_ref[...]
    xn = (x * lax.rsqrt(jnp.mean(x * x, axis=-1, keepdims=True) + cfg.eps) * g1_ref[...]).astype(BF16)

    widths = (("q", qw), ("k", qw), ("v", qw), ("gq", gqk), ("gk", gqk), ("gv", gvw), ("gr", gvw),
              ("ga", d), ("gb", d), ("glr", LANES))
    cols, start = {}, 0
    for name, width in widths:
        cols[name] = slice(start, start + width)
        start += width

    def proj(name):
        return _dot(xn, w_ref[:, cols[name]])

    nrep = qw // LANES
    cos = jnp.tile(cos_ref[...], (1, nrep))
    sin = jnp.tile(sin_ref[...], (1, nrep))
    lane = lax.broadcasted_iota(jnp.int32, (x.shape[0], qw), 1)
    first_half = (lane & (cfg.da_head_dim - 1)) < (cfg.da_head_dim // 2)
    seg = seg_ref[...]
    lane128 = lax.broadcasted_iota(jnp.int32, (x.shape[0], LANES), 1)

    def group_sums(t):
        hi, lo = _split_bf16(t * t)
        return _dot(hi, seg) + _dot(lo, seg)

    def qk_norm_rope(t, ss, gain):
        r = lax.rsqrt(ss * (1.0 / cfg.da_head_dim) + cfg.eps)
        per_tile = LANES // cfg.da_head_dim
        spread = []
        for v in range(nrep):
            tile = r[:, v * per_tile:v * per_tile + 1]
            for g in range(1, per_tile):
                col = v * per_tile + g
                tile = jnp.where(lane128 < g * cfg.da_head_dim, tile, r[:, col:col + 1])
            spread.append(jnp.broadcast_to(tile, (t.shape[0], LANES)))
        y = t * jnp.concatenate(spread, axis=1) * gain
        half = cfg.da_head_dim // 2
        partner = jnp.where(first_half, pltpu.roll(y, qw - half, 1), pltpu.roll(y, half, 1))
        return y * cos + partner * sin

    q_raw = proj("q")
    q = qk_norm_rope(q_raw, group_sums(q_raw), qkg_ref[0:1, :])
    q_out[...] = (q * (cfg.da_head_dim ** -0.5 * math.log2(math.e))).astype(BF16)
    k_raw = proj("k")
    k = qk_norm_rope(k_raw, group_sums(k_raw), qkg_ref[1:2, :])
    for hm in range(qw // cfg.da_head_dim):
        k_out[:, hm // 2, hm % 2, :] = k[:, hm * cfg.da_head_dim:(hm + 1) * cfg.da_head_dim]
    kb_out[...] = k.astype(BF16)
    v = proj("v")
    v_out[...] = v
    vb_out[...] = v.astype(BF16)
    gq_out[...] = proj("gq") * (cfg.gla_dk ** -0.5)
    gk_out[...] = proj("gk")
    gv_out[...] = proj("gv")
    gr = proj("gr")
    gr_out[...] = gr * jax.nn.sigmoid(gr)
    ga_out[...] = jax.nn.sigmoid(proj("ga"))
    gb_out[...] = jax.nn.sigmoid(proj("gb"))
    gate = _dot(proj("glr").astype(BF16), w2_ref[...]) + b2_ref[...]
    la_out[...] = _log_sigmoid(gate) * (1.0 / cfg.gla_gate_tau)


def _in_proj(cfg, x, cos, sin, p, tm):
    t, d = x.shape
    qw, gqk, gvw = cfg.da_w, cfg.gla_qk_w, cfg.gla_v_w
    n_tab = cos.shape[0] // tm
    tok = lambda w: pl.BlockSpec((tm, w), lambda i: (i, 0))
    const = lambda a: pl.BlockSpec(a.shape, lambda i: (0,) * a.ndim, pipeline_mode=pl.Buffered(1))
    tab = pl.BlockSpec((tm, LANES), lambda i: (i % n_tab, 0))
    consts = [p["g1"], p["w_all"]]
    consts2 = [p["qk_gain"], p["seg"], p["w_gate2"], p["b_gate"]]
    outs = [(qw, BF16), (qw, F32), (qw, BF16), (qw, F32), (qw, BF16), (gqk, F32), (gqk, F32), (gvw, F32),
            (gqk, F32), (gvw, F32), (d, F32), (d, F32)]
    cache_shape = {1: (t, cfg.da_heads, 2, cfg.da_head_dim)}
    cache_spec = {n: pl.BlockSpec((tm,) + s[1:], lambda i, nd=len(s): (i,) + (0,) * (nd - 1))
                  for n, s in cache_shape.items()}
    return pl.pallas_call(
        functools.partial(_in_proj_kernel, cfg),
        grid=(t // tm,),
        in_specs=[tok(d)] + [const(a) for a in consts] + [tab, tab] + [const(a) for a in consts2],
        out_specs=[cache_spec.get(n, tok(w)) for n, (w, _) in enumerate(outs)],
        out_shape=[jax.ShapeDtypeStruct(cache_shape.get(n, (t, w)), dt) for n, (w, dt) in enumerate(outs)],
        compiler_params=_params(1),
        name="in_proj",
    )(x, *consts, cos, sin, *consts2)


def _diff_attn_kernel(cfg, past, tq, tk, n_q, heads_per_step, lam_init,
                      q_ref, k_ref, v_ref, lamp_ref, g_ref, o_ref, *scratch):
    qi = pl.program_id(2)
    hd, vd = cfg.da_head_dim, cfg.da_v_dim

    def blocks(q_start):
        n_blocks = (q_start + tq + tk - 1) // tk
        n_full = (q_start + cfg.chunk) // tk
        return n_full, n_blocks

    for hh in range(heads_per_step):
        cols = slice(hh * vd, (hh + 1) * vd)
        q = q_ref[0, :, cols]
        lane = lax.broadcasted_iota(jnp.int32, q.shape, 1)
        zero = jnp.zeros_like(q)
        qs = jnp.concatenate([jnp.where(lane < hd, q, zero), jnp.where(lane >= hd, q, zero)], axis=0)
        refs = (k_ref.at[:, :, cols], v_ref.at[:, :, cols], lamp_ref, g_ref, o_ref.at[:, :, cols]) + scratch
        attend = functools.partial(_attend, cfg, tq, tk, lam_init, refs, qs)
        if n_q <= ATTN_MAX_STATIC_Q_BLOCKS:
            for c in range(n_q):
                pl.when(qi == c)(functools.partial(attend, past + c * tq, *blocks(past + c * tq), True))
        else:
            q_start = past + qi * tq
            attend(q_start, *blocks(q_start), False)


def _attend(cfg, tq, tk, lam_init, refs, qs, q_start, n_full, n_blocks, static):
    k_ref, v_ref, lamp_ref, g_ref, o_ref, s_ref, mpart_ref, mrow_ref, lpart_ref, acc_ref = refs
    shift = int(math.log2(cfg.chunk))
    n_lane_chunks = tk // LANES
    mpart_ref[...] = jnp.full(mpart_ref.shape, -jnp.inf, F32)

    def scores(j, masked):
        ks = j * tk if static else pl.multiple_of(j * tk, tk)
        s = _dot_nt(qs, k_ref[0, pl.ds(ks, tk), :])
        if masked:
            row = lax.broadcasted_iota(jnp.int32, s.shape, 0)
            col = lax.broadcasted_iota(jnp.int32, s.shape, 1)
            qpos = q_start + jnp.where(row >= tq, row - tq, row)
            allowed = lax.shift_right_logical(ks + col, shift) <= lax.shift_right_logical(qpos, shift)
            s = jnp.where(allowed, s, -jnp.inf)
        s_ref[j] = s
        part = s[:, 0:LANES]
        for c in range(1, n_lane_chunks):
            part = jnp.maximum(part, s[:, c * LANES:(c + 1) * LANES])
        mpart_ref[...] = jnp.maximum(mpart_ref[...], part)

    def in_pairs(fn, lo, hi):
        def pair(n, carry):
            fn(lo + 2 * n)
            fn(lo + 2 * n + 1)
            return carry

        lax.fori_loop(0, (hi - lo) // 2, pair, 0)

        @pl.when((hi - lo) % 2 == 1)
        def _():
            fn(hi - 1)

    if static:
        for j in range(n_blocks):
            scores(j, j >= n_full)
    else:
        in_pairs(lambda j: scores(j, False), 0, n_full)
        in_pairs(lambda j: scores(j, True), n_full, n_blocks)

    mrow_ref[...] = jnp.broadcast_to(jnp.max(mpart_ref[...], axis=-1, keepdims=True), mrow_ref.shape)
    lpart_ref[...] = jnp.zeros(lpart_ref.shape, F32)
    acc_ref[...] = jnp.zeros(acc_ref.shape, F32)

    def weigh(j, carry):
        ks = j * tk if static else pl.multiple_of(j * tk, tk)
        m = mrow_ref[...]
        ps = [jnp.exp2(s_ref[j, :, c * LANES:(c + 1) * LANES] - m) for c in range(n_lane_chunks)]
        lsum = ps[0]
        for pc in ps[1:]:
            lsum = lsum + pc
        lpart_ref[...] += lsum
        p = jnp.concatenate([pc.astype(BF16) for pc in ps], axis=1)
        acc_ref[...] += _dot(p, v_ref[0, pl.ds(ks, tk), :])
        return carry

    if static:
        for j in range(n_blocks):
            weigh(j, 0)
    else:
        in_pairs(lambda j: weigh(j, 0), 0, n_blocks)

    lp = lamp_ref[...]
    lam = (jnp.exp(jnp.sum(lp[0:1] * lp[1:2], axis=-1, keepdims=True))
           - jnp.exp(jnp.sum(lp[2:3] * lp[3:4], axis=-1, keepdims=True)) + lam_init)
    o = acc_ref[...] / jnp.sum(lpart_ref[...], axis=-1, keepdims=True)
    o = o[:tq] - lam * o[tq:]
    o = o * lax.rsqrt(jnp.mean(o * o, axis=-1, keepdims=True) + cfg.eps) * g_ref[...] * (1.0 - lam_init)
    o_ref[0] = o.astype(BF16)


def _diff_attn(cfg, q, k, v, lam_p, subln_g, past, tq, tk, lam_init):
    b, lq, w = q.shape
    lk = k.shape[1]
    vd = cfg.da_v_dim
    assert lq % tq == 0 and lk % tk == 0 and tk % LANES == 0 and tq % cfg.chunk == 0 and past % cfg.chunk == 0
    assert past + lq <= lk
    hps = cfg.da_heads if lq // tq <= ATTN_MAX_STATIC_Q_BLOCKS else 1
    kv_spec = pl.BlockSpec((1, lk, hps * vd), lambda bi, h, qi: (bi, 0, h))
    q_spec = pl.BlockSpec((1, tq, hps * vd), lambda bi, h, qi: (bi, qi, h))
    return pl.pallas_call(
        functools.partial(_diff_attn_kernel, cfg, past, tq, tk, lq // tq, hps, lam_init),
        grid=(b, cfg.da_heads // hps, lq // tq),
        in_specs=[q_spec, kv_spec, kv_spec,
                  pl.BlockSpec(lam_p.shape, lambda bi, h, qi: (0, 0)),
                  pl.BlockSpec(subln_g.shape, lambda bi, h, qi: (0, 0))],
        out_specs=q_spec,
        out_shape=jax.ShapeDtypeStruct((b, lq, w), BF16),
        scratch_shapes=[pltpu.VMEM((lk // tk, 2 * tq, tk), F32), pltpu.VMEM((2 * tq, LANES), F32),
                        pltpu.VMEM((2 * tq, LANES), F32), pltpu.VMEM((2 * tq, LANES), F32),
                        pltpu.VMEM((2 * tq, vd), F32)],
        compiler_params=_params(3),
        name="diff_attn",
    )(q, k, v, lam_p, subln_g)


def _gla_constants(c):
    r = np.arange(c)
    i = r[:, None]
    rr = r[None, :]
    mats = [(rr <= i), (rr > i)]
    levels = []
    s = c // 2
    while s >= 1:
        levels.append(s)
        s //= 2
    for s in levels:
        bd = (i // s) * s
        mats.append((rr > bd) & (rr <= i))
        bd2 = (i // s + 1) * s
        mats.append((rr > i) & (rr <= bd2))
    stack = np.concatenate([m.astype(np.float32) for m in mats], axis=0)
    lev = np.full((c, c), -1, np.int32)
    jj = r[None, :]
    for n, s in enumerate(levels):
        sel = (jj < i) & (i // s != jj // s) & (lev < 0)
        lev[sel] = n
    lev[i == jj] = len(levels)
    return stack, lev, len(levels)


def _gla_kernel(cfg, n_levels, n_sub, q_ref, k_ref, v_ref, a_ref, s0_ref, mat_ref, lev_ref, g_ref,
                o_ref, s_out_ref, st_ref):
    c, dk, dv = cfg.chunk, cfg.gla_dk, cfg.gla_dv
    step_i = pl.program_id(1)

    @pl.when(step_i == 0)
    def _():
        for hd in range(cfg.gla_heads):
            st_ref[hd] = s0_ref[0, hd].T

    mat = mat_ref[...]
    lev = lev_ref[...]

    heads = range(cfg.gla_heads)
    kcols = [slice(hd * dk, (hd + 1) * dk) for hd in heads]
    vcols = [slice(hd * dv, (hd + 1) * dv) for hd in heads]

    def chunk(n, carry):
        rows = pl.ds(pl.multiple_of(n * c, c), c)
        a_hi, a_lo = _split_bf16(a_ref[0, rows, :])
        e_all = _dot(mat, a_hi) + _dot(mat, a_lo)
        e = [e_all[:, kcols[hd]] for hd in heads]
        q = [q_ref[0, rows, kcols[hd]] for hd in heads]
        k = [k_ref[0, rows, kcols[hd]] for hd in heads]
        vb = [v_ref[0, rows, vcols[hd]].astype(BF16) for hd in heads]

        att = [jnp.where(lev == n_levels, _dot_nt(q[hd].astype(BF16), k[hd].astype(BF16)), 0.0) for hd in heads]
        for n_l in range(n_levels):
            eq = slice((2 + 2 * n_l) * c, (3 + 2 * n_l) * c)
            ek = slice((3 + 2 * n_l) * c, (4 + 2 * n_l) * c)
            a_l = [_dot_nt((q[hd] * jnp.exp(e[hd][eq])).astype(BF16), (k[hd] * jnp.exp(e[hd][ek])).astype(BF16))
                   for hd in heads]
            att = [jnp.where(lev == n_l, a_l[hd], att[hd]) for hd in heads]

        st = [st_ref[hd] for hd in heads]
        o = [_dot_nt((q[hd] * jnp.exp(e[hd][0:c])).astype(BF16), st[hd].astype(BF16))
             + _dot(att[hd].astype(BF16), vb[hd]) for hd in heads]
        kl = [(k[hd] * jnp.exp(e[hd][c:2 * c])).astype(BF16) for hd in heads]
        upd = [lax.dot_general(vb[hd], kl[hd], (((0,), (0,)), ((), ())), preferred_element_type=F32)
               for hd in heads]
        for hd in heads:
            st_ref[hd] = st[hd] * jnp.exp(e[hd][c - 1:c, :]) + upd[hd]
            on = o[hd] * lax.rsqrt(jnp.mean(o[hd] * o[hd], axis=-1, keepdims=True) + cfg.eps) * g_ref[...]
            o_ref[0, rows, vcols[hd]] = on
        return carry

    lax.fori_loop(0, n_sub, chunk, 0, unroll=2 if n_sub % 2 == 0 else 1)

    @pl.when(step_i == pl.num_programs(1) - 1)
    def _():
        for hd in range(cfg.gla_heads):
            s_out_ref[0, hd] = st_ref[hd].T


def _gla(cfg, gq, gk, gv, la, s0, norm_g, lb):
    b, l, _ = gq.shape
    c, dk, dv, h = cfg.chunk, cfg.gla_dk, cfg.gla_dv, cfg.gla_heads
    assert l % lb == 0 and lb % c == 0
    stack, lev, n_levels = _gla_constants(c)
    mat = jnp.asarray(stack, BF16)
    lev = jnp.asarray(lev)
    seq = lambda w: pl.BlockSpec((1, lb, h * w), lambda bi, si: (bi, si, 0))
    state = pl.BlockSpec((1, h, dk, dv), lambda bi, si: (bi, 0, 0, 0))
    const = lambda a: pl.BlockSpec(a.shape, lambda bi, si: (0,) * a.ndim)
    return pl.pallas_call(
        functools.partial(_gla_kernel, cfg, n_levels, lb // c),
        grid=(b, l // lb),
        in_specs=[seq(dk), seq(dk), seq(dv), seq(dk), state, const(mat), const(lev), const(norm_g)],
        out_specs=[seq(dv), state],
        out_shape=[jax.ShapeDtypeStruct((b, l, h * dv), F32), jax.ShapeDtypeStruct((b, h, dk, dv), F32)],
        scratch_shapes=[pltpu.VMEM((h, dv, dk), F32)],
        compiler_params=_params(2, arbitrary_last=True),
        name="gla",
    )(gq, gk, gv, la, s0, mat, lev, norm_g)


def _merge_kernel(cfg, x_ref, oa_ref, ob_ref, gr_ref, ga_ref, gb_ref, wa_ref, wb_ref, wo_ref, g2_ref,
                  h_ref, hn_ref):
    a = _dot(oa_ref[...], wa_ref[...])
    bb = _dot((ob_ref[...] * gr_ref[...]).astype(BF16), wb_ref[...])
    mix = ga_ref[...] * a + gb_ref[...] * bb
    h = x_ref[...] + _dot(mix.astype(BF16), wo_ref[...])
    h_ref[...] = h
    hn = h * lax.rsqrt(jnp.mean(h * h, axis=-1, keepdims=True) + cfg.eps) * g2_ref[...]
    hn_ref[...] = hn.astype(BF16)


def _merge_out(cfg, x, oa, ob, gr, ga, gb, p, tm):
    t, d = x.shape
    tok = lambda a: pl.BlockSpec((tm, a.shape[1]), lambda i: (i, 0))
    const = lambda a: pl.BlockSpec(a.shape, lambda i: (0,) * a.ndim, pipeline_mode=pl.Buffered(1))
    toks = [x, oa, ob, gr, ga, gb]
    consts = [p["w_a"], p["w_b"], p["w_o"], p["g2"]]
    return pl.pallas_call(
        functools.partial(_merge_kernel, cfg),
        grid=(t // tm,),
        in_specs=[tok(a) for a in toks] + [const(a) for a in consts],
        out_specs=[pl.BlockSpec((tm, d), lambda i: (i, 0))] * 2,
        out_shape=[jax.ShapeDtypeStruct((t, d), F32), jax.ShapeDtypeStruct((t, d), BF16)],
        compiler_params=_params(1),
        name="merge_out",
    )(*toks, *consts)


def _top_values(x, k, with_rank=False):
    vals = []
    rank = jnp.full(x.shape, float(k), F32) if with_rank else None
    for n in range(k):
        m = jnp.max(x, axis=0, keepdims=True)
        vals.append(m)
        hit = x == m
        if with_rank:
            rank = jnp.where(hit, float(n), rank)
        x = jnp.where(hit, -jnp.inf, x)
    return (vals, rank) if with_rank else vals


def _rows_to_block(rows):
    k, t = len(rows), rows[0].shape[1]
    idx = lax.broadcasted_iota(jnp.int32, (k, t), 0)
    out = jnp.zeros((k, t), F32)
    for n, r in enumerate(rows):
        out = jnp.where(idx == n, r, out)
    return out


def _candidate_sums(top0, top1, k):
    tile = 8
    assert k % tile == 0 and k // (tile + 1) == 1
    blk0, blk1 = _rows_to_block(top0), _rows_to_block(top1)
    row = lax.broadcasted_iota(jnp.int32, (tile, top0[0].shape[1]), 0)
    pieces = []
    for b in range(tile):
        n_a = k // (b + 1)
        for a0 in range(0, n_a, tile):
            piece = blk0[a0:a0 + tile] + top1[b]
            pieces.append(piece if a0 + tile <= n_a else jnp.where(row < n_a - a0, piece, -jnp.inf))
    for b0 in range(tile, k, tile):
        pieces.append(blk1[b0:b0 + tile] + top0[0])
    return jnp.concatenate(pieces, axis=0)


def _peer_route_kernel(cfg, hn_ref, wq_ref, keys_ref, c1_ref, r2_ref, e1_ref, e2_ref):
    nk, topk = cfg.peer_n_keys, cfg.peer_topk
    sub = cfg.peer_qdim // 2
    q = _dot(hn_ref[...], wq_ref[...]).astype(BF16)
    for h in range(cfg.peer_heads):
        sc = [_dot_nt(keys_ref[2 * h + half], q[:, (2 * h + half) * sub:(2 * h + half + 1) * sub])
              for half in range(2)]
        top0 = _top_values(sc[0], topk)
        top1, rank1 = _top_values(sc[1], topk, with_rank=True)
        top = [top0, top1]
        best = _top_values(_candidate_sums(top0, top1, topk), topk)
        tau = best[topk - 1]
        z = best[0] * 0.0
        for cv in best:
            z = z + jnp.exp(cv - best[0])
        count = jnp.zeros(sc[0].shape, F32)
        for b in range(topk):
            count = count + jnp.where(sc[0] + top[1][b] >= tau, 1.0, 0.0)
        rows = pl.ds(h * nk, nk)
        c1_ref[rows, :] = count
        r2_ref[rows, :] = rank1
        e1_ref[rows, :] = jnp.exp(sc[0] - top[0][0]) / z
        e2_ref[rows, :] = jnp.exp(sc[1] - top[1][0])


def _peer_route(cfg, hn, wq, keys, tb):
    t, d = hn.shape
    hk = cfg.peer_heads * cfg.peer_n_keys
    const = lambda a: pl.BlockSpec(a.shape, lambda i: (0,) * a.ndim, pipeline_mode=pl.Buffered(1))
    col = lambda r: pl.BlockSpec((r, tb), lambda i: (0, i))
    return pl.pallas_call(
        functools.partial(_peer_route_kernel, cfg),
        grid=(t // tb,),
        in_specs=[pl.BlockSpec((tb, d), lambda i: (i, 0)), const(wq), const(keys)],
        out_specs=[col(hk)] * 4,
        out_shape=[jax.ShapeDtypeStruct((hk, t), F32)] * 4,
        compiler_params=_params(1),
        name="peer_route",
    )(hn, wq, keys)


def _gelu(x):
    return 0.5 * x * (1.0 + lax.erf(x * (2.0 ** -0.5)))


def _peer_dense_kernel(cfg, ib, hn_ref, h_ref, u_ref, v_ref, c1_ref, r2_ref, e1_ref, e2_ref,
                       y_ref, acc_ref, act_ref, p_ref):
    nk = cfg.peer_n_keys
    ei = pl.program_id(1)
    tb = hn_ref.shape[0]

    @pl.when(ei == 0)
    def _():
        acc_ref[...] = jnp.zeros(acc_ref.shape, F32)

    act_ref[...] = _gelu(_dot_nt(u_ref[...], hn_ref[...]))

    n_row_tiles = nk // GATE_ROWS

    def gate_tile(n, carry):
        r0 = pl.multiple_of((n % n_row_tiles) * GATE_ROWS, GATE_ROWS)
        lanes = pl.ds(pl.multiple_of((n // n_row_tiles) * LANES, LANES), LANES)
        for i0 in range(0, ib, GATE_FIRST):
            g = [None] * GATE_FIRST
            for h in range(cfg.peer_heads):
                rows = pl.ds(h * nk + r0, GATE_ROWS)
                r2, e2 = r2_ref[rows, lanes], e2_ref[rows, lanes]
                for k in range(GATE_FIRST):
                    keep = r2 < c1_ref[h, i0 + k:i0 + k + 1, lanes]
                    term = jnp.where(keep, e2, 0.0) * e1_ref[h, i0 + k:i0 + k + 1, lanes]
                    g[k] = term if h == 0 else g[k] + term
            for k in range(GATE_FIRST):
                rows = pl.ds((i0 + k) * nk + r0, GATE_ROWS)
                p_ref[rows, lanes] = (g[k] * act_ref[rows, lanes]).astype(BF16)
        return carry

    n_tiles = n_row_tiles * (tb // LANES)
    lax.fori_loop(0, n_tiles, gate_tile, 0, unroll=2 if n_tiles % 2 == 0 else 1)
    acc_ref[...] += lax.dot_general(v_ref[...], p_ref[...], (((0,), (0,)), ((), ())),
                                    preferred_element_type=F32)

    @pl.when(ei == pl.num_programs(1) - 1)
    def _():
        y_ref[...] = h_ref[...] + acc_ref[...].T


def _peer_dense(cfg, hn, h, u, v, route, tb, ib):
    t, d = hn.shape
    nk = cfg.peer_n_keys
    eb = ib * nk
    hk = cfg.peer_heads * nk
    c1, r2, e1, e2 = route
    by_head = lambda a: a.reshape(cfg.peer_heads, nk, t)
    tokb = pl.BlockSpec((tb, d), lambda ti, ei: (ti, 0))
    col = lambda r: pl.BlockSpec((r, tb), lambda ti, ei: (0, ti))
    first = pl.BlockSpec((cfg.peer_heads, ib, tb), lambda ti, ei: (0, ei, ti))
    return pl.pallas_call(
        functools.partial(_peer_dense_kernel, cfg, ib),
        grid=(t // tb, cfg.n_experts // eb),
        in_specs=[tokb, tokb,
                  pl.BlockSpec((eb, d), lambda ti, ei: (ei, 0)),
                  pl.BlockSpec((eb, d), lambda ti, ei: (ei, 0)),
                  first, col(hk), first, col(hk)],
        out_specs=tokb,
        out_shape=jax.ShapeDtypeStruct((t, d), F32),
        scratch_shapes=[pltpu.VMEM((d, tb), F32), pltpu.VMEM((eb, tb), F32), pltpu.VMEM((eb, tb), BF16)],
        compiler_params=_params(2, arbitrary_last=True),
        name="peer_dense",
    )(hn, h, u, v, by_head(c1), r2, by_head(e1), e2)


def _prep_params(cfg, norm1_g, w_in, qk_norm_g, da_lambda, da_subln_g, gla_w_gate2, gla_b_gate, gla_norm_g,
                 w_branch_a, w_branch_b, w_out, norm2_g, peer_w_q, peer_sub_keys, peer_u, peer_v):
    d, qw, gqk, gvw, rank = cfg.d_model, cfg.da_w, cfg.gla_qk_w, cfg.gla_v_w, cfg.gla_gate_rank
    splits = (qw, qw, qw, gqk, gqk, gvw, rank, gvw, d, d)
    starts = np.concatenate([[0], np.cumsum(splits)])
    piece = lambda n: w_in[:, starts[n]:starts[n + 1]]
    glr = jnp.pad(piece(6), ((0, 0), (0, LANES - rank)))
    w_all = jnp.concatenate([piece(n) for n in (0, 1, 2, 3, 4, 5, 7, 8, 9)] + [glr], axis=1).astype(BF16)
    group = np.arange(qw) // cfg.da_head_dim
    seg = (group[:, None] == np.arange(LANES)[None, :]).astype(np.float32)
    return {
        "g1": norm1_g[None, :], "w_all": w_all,
        "qk_gain": jnp.tile(qk_norm_g, (1, qw // cfg.da_head_dim)),
        "seg": jnp.asarray(seg, BF16),
        "w_gate2": jnp.pad(gla_w_gate2, ((0, LANES - rank), (0, 0))).astype(BF16),
        "b_gate": gla_b_gate[None, :],
        "da_lambda": da_lambda, "subln_g": da_subln_g[None, :], "gla_norm_g": gla_norm_g[None, :],
        "w_a": w_branch_a.astype(BF16), "w_b": w_branch_b.astype(BF16), "w_o": w_out.astype(BF16),
        "g2": norm2_g[None, :],
        "wq": peer_w_q.astype(BF16),
        "keys": peer_sub_keys.reshape(cfg.peer_heads * 2, cfg.peer_n_keys, cfg.peer_qdim // 2).astype(BF16),
        "u": peer_u.astype(BF16), "v": peer_v.astype(BF16),
    }


def _pick(n, target):
    t = min(n, target)
    while n % t:
        t //= 2
    return t


def _layer(cfg, x, cache_k, cache_v, s0, p, lam_init):
    b, l, d = x.shape
    t = b * l
    past = 0 if cache_k is None else cache_k.shape[1]
    qw = cfg.da_w

    tm = _pick(t, 256)
    if l % tm == 0:
        positions = past + jnp.arange(l)
    else:
        positions = past + (jnp.arange(tm) % l)
    cos, sin = _rope_tables(cfg, positions)
    q, k, kb, v, vb, gq, gk, gv, la, gr, ga, gb = _in_proj(cfg, x.reshape(t, d), cos, sin, p, tm)

    r3 = lambda a: a.reshape(b, l, a.shape[-1])
    if cache_k is None:
        kk, vv = r3(kb), r3(vb)
        tq = _pick(l, ATTN_TQ)
        tk = _pick(l, ATTN_TK)
    else:
        tq = _pick(l, ATTN_TQ)
        tk = -(-(past + l) // LANES) * LANES
        pad = jnp.zeros((b, tk - past - l, qw), BF16)
        kk = jnp.concatenate([cache_k.reshape(b, past, qw).astype(BF16), r3(kb), pad], axis=1)
        vv = jnp.concatenate([cache_v.reshape(b, past, qw).astype(BF16), r3(vb), pad], axis=1)
    oa = _diff_attn(cfg, r3(q), kk, vv, p["da_lambda"], p["subln_g"], past, tq, tk, lam_init)

    if s0 is None:
        s0 = jnp.zeros((b, cfg.gla_heads, cfg.gla_dk, cfg.gla_dv), F32)
    ob, s_new = _gla(cfg, r3(gq), r3(gk), r3(gv), r3(la), s0, p["gla_norm_g"], _pick(l, 512))

    h, hn = _merge_out(cfg, x.reshape(t, d), oa.reshape(t, qw), ob.reshape(t, -1), gr, ga, gb, p, tm)

    tb = _pick(t, 512)
    route = _peer_route(cfg, hn, p["wq"], p["keys"], _pick(t, ROUTE_TOKENS))
    y = _peer_dense(cfg, hn, h, p["u"], p["v"], route, tb, PEER_FIRST_PER_STEP)

    k_new = k.reshape(b, l, cfg.da_heads, 2, cfg.da_head_dim)
    v_new = v.reshape(b, l, cfg.da_heads, cfg.da_v_dim)
    return y.reshape(b, l, d), k_new, v_new, s_new


def _forward(cfg, x_prompt, x_sample, cache_k, cache_v, state_gla, *weights):
    yp, ys = x_prompt, x_sample
    outs = [[] for _ in range(6)]
    for layer in range(cache_k.shape[0]):
        p = _prep_params(cfg, *[w[layer] for w in weights])
        lam_init = 0.8 - 0.6 * math.exp(-0.3 * layer)
        yp, kp, vp, sp = _layer(cfg, yp, None, None, None, p, lam_init)
        ys, kn, vn, sn = _layer(cfg, ys, cache_k[layer], cache_v[layer], state_gla[layer], p, lam_init)
        for lst, a in zip(outs, (kp, vp, sp, kn, vn, sn)):
            lst.append(a)
    return (yp, ys) + tuple(jnp.stack(o) for o in outs)


def kernel(x_prompt, x_sample, cache_k, cache_v, state_gla, norm1_g, w_in, qk_norm_g, da_lambda, da_subln_g,
           gla_w_gate2, gla_b_gate, gla_norm_g, w_branch_a, w_branch_b, w_out, norm2_g, peer_w_q,
           peer_sub_keys, peer_u, peer_v):
    return _forward(Cfg(), x_prompt, x_sample, cache_k, cache_v, state_gla, norm1_g, w_in, qk_norm_g, da_lambda,
                    da_subln_g, gla_w_gate2, gla_b_gate, gla_norm_g, w_branch_a, w_branch_b, w_out, norm2_g,
                    peer_w_q, peer_sub_keys, peer_u, peer_v)
```
